```python
import math
import jax, jax.numpy as jnp
from jax import lax
import numpy as np


D_MODEL = 1024
BATCH = 16
SEQ = 2048
DEPTH = 4

N_EVEN = (DEPTH + 1) // 2
N_ODD = DEPTH // 2
HEAD_DIM = 64
EPS = 1e-6

A_WIDTH = D_MODEL // 2
A_EXPAND = 128
A_HEADS = A_WIDTH // A_EXPAND
A_HEAD_V = A_WIDTH // A_HEADS
A_CHUNK = 32

B_WIDTH = D_MODEL - A_WIDTH
B_HEADS = B_WIDTH // HEAD_DIM
DILATED_PATTERNS = ((128, 1), (512, 4), (2048, 16))
B_QBLOCK = 128
EVEN_IN = 4 * A_WIDTH + 3 * B_WIDTH

C_HEADS = D_MODEL // HEAD_DIM
C_KV_HEADS = 4
C_GROUP = C_HEADS // C_KV_HEADS
C_KV_WIDTH = C_KV_HEADS * HEAD_DIM
CMP_LEN = 32
CMP_STRIDE = 16
CMP_HIDDEN = 2 * HEAD_DIM
SLC_BLOCK = 64
SLC_TOPN = 8
SLC_QBLOCK = 32
WIN = 512
WIN_QBLOCK = 128
ODD_IN = D_MODEL + 6 * C_KV_WIDTH + 3 * C_HEADS

D_FF = -(-8 * D_MODEL // (3 * 256)) * 256

kernel_name = 'hybrid_hgrn2_dilated_nsa_trunk'


def rms_norm(x, g):
    xf = x.astype(jnp.float32)
    y = xf * lax.rsqrt(jnp.mean(xf * xf, axis=-1, keepdims=True) + EPS)
    return (y * g.astype(jnp.float32)).astype(x.dtype)


def alibi_slopes(n):
    return 2.0 ** (-8.0 * jnp.arange(1, n + 1, dtype=jnp.float32) / n)


def hgrn2(q, f_logit, v, gate, lb, o_gain):
    Bn, S, _ = q.shape
    f32 = jnp.float32
    f = lb + (1.0 - lb) * jax.nn.sigmoid(f_logit.astype(f32))
    log_f = jnp.log(f)
    k = 1.0 - f
    q = jax.nn.silu(q.astype(f32))
    v = v.astype(f32)
    n_chunk = S // A_CHUNK

    def chunks(t, d):
        return t.reshape(Bn, n_chunk, A_CHUNK, A_HEADS, d).transpose(1, 0, 3, 2, 4)

    qc, kc, lc = chunks(q, A_EXPAND), chunks(k, A_EXPAND), chunks(log_f, A_EXPAND)
    vc = chunks(v, A_HEAD_V)
    causal = jnp.tril(jnp.ones((A_CHUNK, A_CHUNK), dtype=bool))[:, :, None]

    def step(state, inp):
        qb, kb, vb, lf = inp
        b = jnp.cumsum(lf, axis=-2)
        b_last = b[..., -1:, :]
        diff = b[..., :, None, :] - b[..., None, :, :]
        decay = jnp.exp(jnp.where(causal, diff, -jnp.inf))
        scores = jnp.einsum('bhtk,bhsk,bhtsk->bhts', qb, kb, decay)
        out = (jnp.einsum('bhts,bhsv->bhtv', scores, vb)
               + jnp.einsum('bhtk,bhkv->bhtv', qb * jnp.exp(b), state))
        new_state = (jnp.exp(b_last)[..., 0, :, None] * state
                     + jnp.einsum('bhsk,bhsv->bhkv', kb * jnp.exp(b_last - b), vb))
        return new_state, out

    state0 = jnp.zeros((Bn, A_HEADS, A_EXPAND, A_HEAD_V), f32)
    _, o = lax.scan(step, state0, (qc, kc, vc, lc))
    o = o.transpose(1, 0, 3, 2, 4).reshape(Bn, S, A_HEADS, A_HEAD_V)
    o = o * lax.rsqrt(jnp.mean(o * o, axis=-1, keepdims=True) + EPS)
    return o.reshape(Bn, S, A_WIDTH) * o_gain.astype(f32) * jax.nn.silu(gate.astype(f32))


def dilated_branch(q, k, v, window, dilation, slopes):
    Bn, H, S, dh = q.shape
    L = S // dilation
    w = window // dilation
    qb = min(B_QBLOCK, L)
    nb = L // qb

    def sub(t):
        return t.reshape(Bn, H, L, dilation, dh).transpose(0, 1, 3, 2, 4)

    pad = ((0, 0), (0, 0), (0, 0), (w, 0), (0, 0))
    kp = jnp.pad(sub(k), pad)
    vp = jnp.pad(sub(v), pad)
    kidx = jnp.arange(nb)[:, None] * qb + jnp.arange(qb + w)[None, :]
    k_band = kp[:, :, :, kidx]
    v_band = vp[:, :, :, kidx]
    q_blk = sub(q).reshape(Bn, H, dilation, nb, qb, dh)
    j = jnp.arange(qb)[:, None] - jnp.arange(qb + w)[None, :] + w
    valid = (j >= 0) & (j <= w) & (kidx[:, None, :] >= w)
    s = (jnp.einsum('bhrnqd,bhrnkd->bhrnqk', q_blk, k_band) * dh ** -0.5
         - slopes[:, None, None, None, None] * (j * dilation).astype(jnp.float32))
    s = jnp.where(valid, s, -jnp.inf)
    m = jnp.max(s, axis=-1, keepdims=True)
    e = jnp.exp(s - m)
    l = jnp.sum(e, axis=-1, keepdims=True)
    o = jnp.einsum('bhrnqk,bhrnkd->bhrnqd', e, v_band) / l
    lse = (m + jnp.log(l))[..., 0]
    o = o.reshape(Bn, H, dilation, L, dh).transpose(0, 1, 3, 2, 4).reshape(Bn, H, S, dh)
    lse = lse.reshape(Bn, H, dilation, L).transpose(0, 1, 3, 2).reshape(Bn, H, S)
    return o, lse


def hybrid_mixer(h, w_in, lb, o_gain, q_gain, k_gain, w_out):
    Bn, S, _ = h.shape
    dt = h.dtype
    proj = h @ w_in
    split_at = [A_WIDTH * i for i in range(1, 5)] + [4 * A_WIDTH + B_WIDTH * i for i in range(1, 3)]
    a_q, a_f, a_i, a_g, b_q, b_k, b_v = jnp.split(proj, split_at, axis=-1)
    a_out = hgrn2(a_q, a_f, a_i, a_g, lb, o_gain)

    def heads(t):
        return t.reshape(Bn, S, B_HEADS, HEAD_DIM).transpose(0, 2, 1, 3).astype(jnp.float32)

    q = rms_norm(heads(b_q), q_gain)
    k = rms_norm(heads(b_k), k_gain)
    v = heads(b_v)
    slopes = alibi_slopes(B_HEADS)
    outs, lses = zip(*[dilated_branch(q, k, v, win, dil, slopes) for (win, dil) in DILATED_PATTERNS])
    wts = jax.nn.softmax(jnp.stack(lses), axis=0)
    b_out = jnp.einsum('nbhs,nbhsd->bshd', wts, jnp.stack(outs)).reshape(Bn, S, B_WIDTH)
    mixed = jnp.concatenate([a_out, b_out], axis=-1).astype(dt)
    return mixed @ w_out


def nsa_mixer(h, w_in, q_gain, k_gain, cmp_pe, cmp_w1, cmp_w2, w_out):
    Bn, S, _ = h.shape
    dt = h.dtype
    f32 = jnp.float32
    scale = HEAD_DIM ** -0.5
    proj = h @ w_in
    q, kc, vc, ks, vs, kw, vw, g = jnp.split(
        proj, [D_MODEL + C_KV_WIDTH * i for i in range(7)], axis=-1)
    q = rms_norm(q.reshape(Bn, S, C_KV_HEADS, C_GROUP, HEAD_DIM)
                 .transpose(0, 2, 3, 1, 4).astype(f32), q_gain)

    def kv_heads(t):
        return t.reshape(Bn, S, C_KV_HEADS, HEAD_DIM).transpose(0, 2, 1, 3).astype(f32)

    kc, vc, ks, vs, kw, vw = (kv_heads(t) for t in (kc, vc, ks, vs, kw, vw))
    ks = rms_norm(ks, k_gain[1])
    kw = rms_norm(kw, k_gain[2])
    gates = jax.nn.sigmoid(g.astype(f32)).reshape(Bn, S, 3, C_KV_HEADS, C_GROUP).transpose(2, 0, 3, 4, 1)[..., None]
    slopes = alibi_slopes(C_HEADS).reshape(C_KV_HEADS, C_GROUP)
    pos = jnp.arange(S)

    n_cmp = (S - CMP_LEN) // CMP_STRIDE + 1
    cidx = jnp.arange(n_cmp)[:, None] * CMP_STRIDE + jnp.arange(CMP_LEN)[None, :]

    def compress(t, pe, w1, w2):
        blocks = t[:, :, cidx] + pe.astype(f32)
        blocks = blocks.reshape(Bn, C_KV_HEADS, n_cmp, CMP_LEN * HEAD_DIM)
        return jax.nn.silu(blocks @ w1.astype(f32)) @ w2.astype(f32)

    k_cmp = rms_norm(compress(kc, cmp_pe[0], cmp_w1[0], cmp_w2[0]), k_gain[0])
    v_cmp = compress(vc, cmp_pe[1], cmp_w1[1], cmp_w2[1])
    dist_c = (pos[:, None] - cidx[:, -1][None, :]).astype(f32)
    s_c = (jnp.einsum('bgjtd,bgnd->bgjtn', q, k_cmp) * scale
           - slopes[None, :, :, None, None] * dist_c)
    s_c = jnp.where(dist_c >= 0, s_c, -jnp.inf)
    m_c = jnp.max(s_c, axis=-1, keepdims=True)
    e_c = jnp.exp(s_c - jnp.where(jnp.isfinite(m_c), m_c, 0.0))
    l_c = jnp.sum(e_c, axis=-1, keepdims=True)
    p_c = e_c / jnp.where(l_c > 0, l_c, 1.0)
    o_cmp = jnp.einsum('bgjtn,bgnd->bgjtd', p_c, v_cmp)

    n_slc = S // SLC_BLOCK
    blk = jnp.arange(n_slc)
    c_start = cidx[:, 0]
    overlap = ((c_start[:, None] < (blk[None, :] + 1) * SLC_BLOCK)
               & (c_start[:, None] + CMP_LEN > blk[None, :] * SLC_BLOCK)).astype(f32)
    importance = jnp.einsum('bgjtn,nm->bgtm', p_c, overlap)
    cur = (pos // SLC_BLOCK)[:, None]
    forced = (blk[None, :] == 0) | (blk[None, :] == cur) | (blk[None, :] == cur - 1)
    score = jnp.where(blk[None, :] > cur, -jnp.inf, jnp.where(forced, jnp.inf, importance))
    n_top = min(SLC_TOPN, n_slc)
    top_s, top_i = lax.top_k(score, n_top)
    top_ok = top_s > -jnp.inf
    k_blocks = ks.reshape(Bn, C_KV_HEADS, n_slc, SLC_BLOCK, HEAD_DIM)
    v_blocks = vs.reshape(Bn, C_KV_HEADS, n_slc, SLC_BLOCK, HEAD_DIM)
    gather = jax.vmap(jax.vmap(lambda tb, idx: tb[idx]))
    n_sel_keys = n_top * SLC_BLOCK

    def sel_block(args):
        qb, ib, okb, tb = args
        kg = gather(k_blocks, ib).reshape(Bn, C_KV_HEADS, SLC_QBLOCK, n_sel_keys, HEAD_DIM)
        vg = gather(v_blocks, ib).reshape(Bn, C_KV_HEADS, SLC_QBLOCK, n_sel_keys, HEAD_DIM)
        kpos = (ib[..., None] * SLC_BLOCK + jnp.arange(SLC_BLOCK)).reshape(Bn, C_KV_HEADS, SLC_QBLOCK, n_sel_keys)
        ok = jnp.broadcast_to(okb[..., None], ib.shape + (SLC_BLOCK,)).reshape(Bn, C_KV_HEADS, SLC_QBLOCK, n_sel_keys)
        dist = tb[:, None] - kpos
        valid = (dist >= 0) & ok
        s = (jnp.einsum('bgjqd,bgqmd->bgjqm', qb, kg) * scale
             - slopes[None, :, :, None, None] * dist[:, :, None].astype(f32))
        s = jnp.where(valid[:, :, None], s, -jnp.inf)
        return jnp.einsum('bgjqm,bgqmd->bgjqd', jax.nn.softmax(s, axis=-1), vg)

    n_qb = S // SLC_QBLOCK
    q_r = q.reshape(Bn, C_KV_HEADS, C_GROUP, n_qb, SLC_QBLOCK, HEAD_DIM).transpose(3, 0, 1, 2, 4, 5)
    i_r = top_i.reshape(Bn, C_KV_HEADS, n_qb, SLC_QBLOCK, n_top).transpose(2, 0, 1, 3, 4)
    ok_r = top_ok.reshape(Bn, C_KV_HEADS, n_qb, SLC_QBLOCK, n_top).transpose(2, 0, 1, 3, 4)
    t_r = pos.reshape(n_qb, SLC_QBLOCK)
    o_slc = lax.map(sel_block, (q_r, i_r, ok_r, t_r)).transpose(1, 2, 3, 0, 4, 5).reshape(
        Bn, C_KV_HEADS, C_GROUP, S, HEAD_DIM)

    pad = ((0, 0), (0, 0), (WIN, 0), (0, 0))
    kw_p = jnp.pad(kw, pad)
    vw_p = jnp.pad(vw, pad)
    band = jnp.arange(WIN_QBLOCK + WIN)

    def win_block(args):
        qb, start = args
        kb = lax.dynamic_slice_in_dim(kw_p, start, WIN_QBLOCK + WIN, axis=2)
        vb = lax.dynamic_slice_in_dim(vw_p, start, WIN_QBLOCK + WIN, axis=2)
        qpos = start + jnp.arange(WIN_QBLOCK)
        kpos = start - WIN + band
        dist = qpos[:, None] - kpos[None, :]
        valid = (dist >= 0) & (dist < WIN) & (kpos[None, :] >= 0)
        s = (jnp.einsum('bgjqd,bgkd->bgjqk', qb, kb) * scale
             - slopes[None, :, :, None, None] * dist.astype(f32))
        s = jnp.where(valid, s, -jnp.inf)
        return jnp.einsum('bgjqk,bgkd->bgjqd', jax.nn.softmax(s, axis=-1), vb)

    n_wb = S // WIN_QBLOCK
    q_w = q.reshape(Bn, C_KV_HEADS, C_GROUP, n_wb, WIN_QBLOCK, HEAD_DIM).transpose(3, 0, 1, 2, 4, 5)
    starts = jnp.arange(n_wb) * WIN_QBLOCK
    o_win = lax.map(win_block, (q_w, starts)).transpose(1, 2, 3, 0, 4, 5).reshape(
        Bn, C_KV_HEADS, C_GROUP, S, HEAD_DIM)

    o = gates[0] * o_cmp + gates[1] * o_slc + gates[2] * o_win
    o = o.transpose(0, 3, 1, 2, 4).reshape(Bn, S, D_MODEL).astype(dt)
    return o @ w_out


def swiglu(h, w_gate, w_up, w_down):
    return (jax.nn.silu(h @ w_gate) * (h @ w_up)) @ w_down


def setup_inputs(seed: int = 0) -> dict:
    key = jax.random.key(seed)
    ks = jax.random.split(key, 20)

    def nrm(k, shape, scale):
        return jax.random.normal(k, shape, jnp.float32) * scale

    def gain(k, shape):
        return 1.0 + 0.02 * jax.random.normal(k, shape, jnp.float32)

    return {
        'x': nrm(ks[0], (BATCH, SEQ, D_MODEL), 1.0),
        'attn_norm': gain(ks[1], (DEPTH, D_MODEL)),
        'ffn_norm': gain(ks[2], (DEPTH, D_MODEL)),
        'hy_w_in': nrm(ks[3], (N_EVEN, D_MODEL, EVEN_IN), D_MODEL ** -0.5),
        'hy_lb_logits': nrm(ks[4], (N_EVEN, A_WIDTH), 0.1),
        'hy_o_gain': gain(ks[5], (N_EVEN, A_WIDTH)),
        'hy_q_gain': gain(ks[6], (N_EVEN, HEAD_DIM)),
        'hy_k_gain': gain(ks[7], (N_EVEN, HEAD_DIM)),
        'hy_w_out': nrm(ks[8], (N_EVEN, D_MODEL, D_MODEL), D_MODEL ** -0.5),
        'nsa_w_in': nrm(ks[9], (N_ODD, D_MODEL, ODD_IN), D_MODEL ** -0.5),
        'nsa_q_gain': gain(ks[10], (N_ODD, HEAD_DIM)),
        'nsa_k_gain': gain(ks[11], (N_ODD, 3, HEAD_DIM)),
        'nsa_cmp_pe': nrm(ks[12], (N_ODD, 2, CMP_LEN, HEAD_DIM), 0.1),
        'nsa_cmp_w1': nrm(ks[13], (N_ODD, 2, CMP_LEN * HEAD_DIM, CMP_HIDDEN), (CMP_LEN * HEAD_DIM) ** -0.5),
        'nsa_cmp_w2': nrm(ks[14], (N_ODD, 2, CMP_HIDDEN, HEAD_DIM), CMP_HIDDEN ** -0.5),
        'nsa_w_out': nrm(ks[15], (N_ODD, D_MODEL, D_MODEL), D_MODEL ** -0.5),
        'ffn_w_gate': nrm(ks[16], (DEPTH, D_MODEL, D_FF), D_MODEL ** -0.5),
        'ffn_w_up': nrm(ks[17], (DEPTH, D_MODEL, D_FF), D_MODEL ** -0.5),
        'ffn_w_down': nrm(ks[18], (DEPTH, D_FF, D_MODEL), D_FF ** -0.5),
    }


def reference(x, attn_norm, ffn_norm, hy_w_in, hy_lb_logits, hy_o_gain, hy_q_gain, hy_k_gain, hy_w_out,
              nsa_w_in, nsa_q_gain, nsa_k_gain, nsa_cmp_pe, nsa_cmp_w1, nsa_cmp_w2, nsa_w_out,
              ffn_w_gate, ffn_w_up, ffn_w_down):
    p_lb = jax.nn.softmax(hy_lb_logits.astype(jnp.float32), axis=0)
    lower_bounds = jnp.cumsum(p_lb, axis=0) - p_lb[0]
    for layer in range(DEPTH):
        h = rms_norm(x, attn_norm[layer])
        i = layer // 2
        if layer % 2 == 0:
            x = x + hybrid_mixer(h, hy_w_in[i], lower_bounds[i], hy_o_gain[i], hy_q_gain[i],
                                 hy_k_gain[i], hy_w_out[i])
        else:
            x = x + nsa_mixer(h, nsa_w_in[i], nsa_q_gain[i], nsa_k_gain[i], nsa_cmp_pe[i],
                              nsa_cmp_w1[i], nsa_cmp_w2[i], nsa_w_out[i])
        h = rms_norm(x, ffn_norm[layer])
        x = x + swiglu(h, ffn_w_gate[layer], ffn_w_up[layer], ffn_w_down[layer])
    return x
```

```python
import functools

import numpy as np
import jax
import jax.numpy as jnp
from jax import lax
from jax.experimental import pallas as pl
from jax.experimental.pallas import tpu as pltpu

F32 = jnp.float32
BF16 = jnp.bfloat16

EPS = 1e-6
HEAD_DIM = 64
ATTN_SCALE = HEAD_DIM ** -0.5
NEG = -1e30

A_EXPAND = 128
A_CHUNK = 32

DILATED_PATTERNS = ((128, 1), (512, 4), (2048, 16))
ATT_TILE = 128

C_KV_HEADS = 4
C_GROUP = 4
CMP_LEN = 32
CMP_STRIDE = 16
CMP_HIDDEN = 2 * HEAD_DIM
SLC_BLOCK = 64
SLC_TOPN = 8
WIN = 512
NSA_GROUP_COLS = 768
NSA_GATE_COL = 640

V7X_VMEM_LIMIT = 56 * 1024 * 1024


def _cparams(*sem):
    return pltpu.CompilerParams(dimension_semantics=sem, vmem_limit_bytes=V7X_VMEM_LIMIT)


def _rms(x, gain):
    return x * lax.rsqrt(jnp.mean(x * x, axis=-1, keepdims=True) + EPS) * gain


def _silu(x):
    return x * jax.nn.sigmoid(x)


def _split3(x):
    hi = x.astype(BF16)
    r1 = x - hi.astype(F32)
    mid = r1.astype(BF16)
    lo = (r1 - mid.astype(F32)).astype(BF16)
    return hi, mid, lo


def _dot(a, b):
    return jnp.dot(a, b, preferred_element_type=F32)


def _dot_nt(a, b):
    return lax.dot_general(a, b, (((1,), (1,)), ((), ())), preferred_element_type=F32)


def _dot_tn(a, b):
    return lax.dot_general(a, b, (((0,), (0,)), ((), ())), preferred_element_type=F32)


def _norm_matmul_kernel(x_ref, g_ref, w_ref, o_ref, h_ref):
    @pl.when(pl.program_id(1) == 0)
    def _():
        h_ref[...] = _rms(x_ref[...], g_ref[...]).astype(BF16)

    o_ref[...] = _dot(h_ref[...], w_ref[...])


def _norm_matmul(x, gain, w, *, tm, tn):
    t, d = x.shape
    n = w.shape[1]
    return pl.pallas_call(
        _norm_matmul_kernel,
        grid=(t // tm, n // tn),
        in_specs=[pl.BlockSpec((tm, d), lambda i, j: (i, 0)),
                  pl.BlockSpec((1, d), lambda i, j: (0, 0)),
                  pl.BlockSpec((d, tn), lambda i, j: (0, j))],
        out_specs=pl.BlockSpec((tm, tn), lambda i, j: (i, j)),
        out_shape=jax.ShapeDtypeStruct((t, n), F32),
        scratch_shapes=[pltpu.VMEM((tm, d), BF16)],
        compiler_params=_cparams("parallel", "arbitrary"),
        name="norm_matmul",
    )(x, gain.reshape(1, d), w)


def _proj_residual_kernel(*refs, n_in):
    x_ref, o_ref = refs[0], refs[-1]
    acc = x_ref[...]
    for a_ref, w_ref in zip(refs[1:1 + n_in], refs[1 + n_in:1 + 2 * n_in]):
        acc = acc + _dot(a_ref[...], w_ref[...])
    o_ref[...] = acc


def _proj_residual(x, parts, weights, *, tm):
    t, d = x.shape
    n_in = len(parts)
    in_specs = [pl.BlockSpec((tm, d), lambda i: (i, 0))]
    in_specs += [pl.BlockSpec((tm, p.shape[1]), lambda i: (i, 0)) for p in parts]
    in_specs += [pl.BlockSpec(w.shape, lambda i: (0, 0)) for w in weights]
    return pl.pallas_call(
        functools.partial(_proj_residual_kernel, n_in=n_in),
        grid=(t // tm,),
        in_specs=in_specs,
        out_specs=pl.BlockSpec((tm, d), lambda i: (i, 0)),
        out_shape=jax.ShapeDtypeStruct((t, d), F32),
        compiler_params=_cparams("parallel"),
        name="proj_residual",
    )(x, *parts, *weights)


def _ffn_kernel(x_ref, g_ref, wg_ref, wu_ref, wd_ref, o_ref, h_ref, acc_ref):
    f = pl.program_id(1)

    @pl.when(f == 0)
    def _():
        x = x_ref[...]
        h_ref[...] = _rms(x, g_ref[...]).astype(BF16)
        acc_ref[...] = x

    h = h_ref[...]
    act = _silu(_dot(h, wg_ref[...])) * _dot(h, wu_ref[...])
    acc_ref[...] += _dot(act.astype(BF16), wd_ref[...])

    @pl.when(f == pl.num_programs(1) - 1)
    def _():
        o_ref[...] = acc_ref[...]


def _ffn(x, gain, w_gate, w_up, w_down, *, tm, tf):
    t, d = x.shape
    dff = w_gate.shape[1]
    return pl.pallas_call(
        _ffn_kernel,
        grid=(t // tm, dff // tf),
        in_specs=[pl.BlockSpec((tm, d), lambda i, f: (i, 0)),
                  pl.BlockSpec((1, d), lambda i, f: (0, 0)),
                  pl.BlockSpec((d, tf), lambda i, f: (0, f)),
                  pl.BlockSpec((d, tf), lambda i, f: (0, f)),
                  pl.BlockSpec((tf, d), lambda i, f: (f, 0))],
        out_specs=pl.BlockSpec((tm, d), lambda i, f: (i, 0)),
        out_shape=jax.ShapeDtypeStruct((t, d), F32),
        scratch_shapes=[pltpu.VMEM((tm, d), BF16), pltpu.VMEM((tm, d), F32)],
        compiler_params=_cparams("parallel", "arbitrary"),
        name="ffn",
    )(x, gain.reshape(1, d), w_gate, w_up, w_down)


def _hgrn2_kernel(q_ref, f_ref, v_ref, g_ref, lbl_ref, og_ref, o_ref, state_ref, p_ref, *, layer_idx, seq):
    c_len, hd = A_CHUNK, A_EXPAND
    logits = lbl_ref[...]
    e = jnp.exp(logits - jnp.max(logits, axis=0, keepdims=True))
    prob = e / jnp.sum(e, axis=0, keepdims=True)
    lb = jnp.sum(prob[:layer_idx + 1], axis=0, keepdims=True) - prob[0:1]
    og = og_ref[...]

    row = lax.broadcasted_iota(jnp.int32, (c_len, hd), 0)
    lane = lax.broadcasted_iota(jnp.int32, (c_len, hd), 1)
    tril = (lax.broadcasted_iota(jnp.int32, (c_len, c_len), 0)
            >= lax.broadcasted_iota(jnp.int32, (c_len, c_len), 1)).astype(BF16)
    ones = jnp.ones((hd, hd), BF16)

    state_ref[...] = jnp.zeros_like(state_ref)
    p_ref[...] = jnp.zeros_like(p_ref)

    def chunk(c, carry):
        rows = pl.ds(pl.multiple_of(c * c_len, c_len), c_len)
        f = lb + (1.0 - lb) * jax.nn.sigmoid(f_ref[rows, :])
        logf = jnp.log(f)
        kk = 1.0 - f
        qq = _silu(q_ref[rows, :])
        v = v_ref[rows, :]
        hi, mid, lo = _split3(logf)
        b = _dot(tril, hi) + _dot(tril, mid) + _dot(tril, lo)
        for s in range(c_len):
            t0 = (s // 8) * 8
            d = b[t0:] - b[s:s + 1]
            e_s = jnp.exp(jnp.where(row[t0:] >= s, d, NEG))
            p_ref[s * c_len + t0:(s + 1) * c_len, :] = qq[t0:] * e_s * kk[s:s + 1]
        r = _dot(p_ref[...].astype(BF16), ones)
        scores = jnp.zeros((c_len, hd), F32)
        for s in range(c_len):
            scores = jnp.where(lane == s, r[s * c_len:(s + 1) * c_len], scores)
        o = _dot(scores[:, :c_len].astype(BF16), v.astype(BF16))
        st = state_ref[...]
        o = o + _dot_nt((qq * jnp.exp(b)).astype(BF16), st.astype(BF16))
        b_last = b[c_len - 1:c_len]
        k_dec = (kk * jnp.exp(b_last - b)).astype(BF16)
        state_ref[...] = st * jnp.exp(b_last) + _dot_tn(v.astype(BF16), k_dec)
        o = o * lax.rsqrt(jnp.mean(o * o, axis=-1, keepdims=True) + EPS)
        o_ref[rows, :] = (o * og * _silu(g_ref[rows, :])).astype(BF16)
        return carry

    lax.fori_loop(0, seq // c_len, chunk, 0)


def _hgrn2(proj, lb_logits, o_gain, *, layer_idx, batch, seq):
    heads = o_gain.shape[0] // A_EXPAND
    n_layers = lb_logits.shape[0]

    def col(off):
        return pl.BlockSpec((seq, A_EXPAND), lambda b, h: (b, h + off * heads))

    return pl.pallas_call(
        functools.partial(_hgrn2_kernel, layer_idx=layer_idx, seq=seq),
        grid=(batch, heads),
        in_specs=[col(0), col(1), col(2), col(3),
                  pl.BlockSpec((n_layers, A_EXPAND), lambda b, h: (0, h)),
                  pl.BlockSpec((1, A_EXPAND), lambda b, h: (0, h))],
        out_specs=pl.BlockSpec((seq, A_EXPAND), lambda b, h: (b, h)),
        out_shape=jax.ShapeDtypeStruct((batch * seq, heads * A_EXPAND), BF16),
        scratch_shapes=[pltpu.VMEM((A_EXPAND, A_EXPAND), F32),
                        pltpu.VMEM((A_CHUNK * A_CHUNK, A_EXPAND), F32)],
        compiler_params=_cparams("parallel", "parallel"),
        name="hgrn2",
    )(proj, proj, proj, proj, lb_logits, o_gain.reshape(1, -1))


def _dilated_log_weights():
    r = np.arange(ATT_TILE)[:, None]
    c = np.arange(ATT_TILE)[None, :]
    out = []
    for delta in (0, 1, 2, 4, 5):
        d = delta * ATT_TILE + r - c
        mult = np.zeros_like(d)
        for window, dilation in DILATED_PATTERNS:
            mult += (d >= 0) & (d % dilation == 0) & (d <= window)
        out.append(np.where(mult > 0, np.log(np.maximum(mult, 1)), NEG))
    return np.stack(out).astype(np.float32)


def _online_softmax_step(a, v_tile, m, l, acc):
    m_new = jnp.maximum(m, jnp.max(a, axis=-1, keepdims=True))
    alpha = jnp.exp(m - m_new)
    p = jnp.exp(a - m_new)
    l = alpha * l + jnp.sum(p, axis=-1, keepdims=True)
    acc = alpha * acc + _dot(p.astype(BF16), v_tile)
    return m_new, l, acc


def _dilated_kernel(slopes_ref, q_ref, k_ref, v_ref, qg_ref, kg_ref, lw_ref, o_ref,
                    qn_ref, kn_ref, vb_ref, *, seq):
    tile = ATT_TILE
    pair = pl.program_id(1)
    n_tiles = seq // tile
    rc = (lax.broadcasted_iota(jnp.int32, (tile, tile), 0)
          - lax.broadcasted_iota(jnp.int32, (tile, tile), 1)).astype(F32)

    for hh in range(2):
        lanes = slice(hh * HEAD_DIM, (hh + 1) * HEAD_DIM)
        qn_ref[hh] = (_rms(q_ref[:, lanes], qg_ref[...]) * ATTN_SCALE).astype(BF16)
        kn_ref[hh] = _rms(k_ref[:, lanes], kg_ref[...]).astype(BF16)
        vb_ref[hh] = v_ref[:, lanes].astype(BF16)

    def q_tile(i, carry):
        outs = []
        for hh in range(2):
            slope = slopes_ref[2 * pair + hh]
            q = qn_ref[hh, pl.ds(pl.multiple_of(i * tile, tile), tile), :]

            def scores(j, cls, delta):
                rows = pl.ds(pl.multiple_of(j * tile, tile), tile)
                s = _dot_nt(q, kn_ref[hh, rows, :])
                dist = rc + jnp.asarray(delta * tile).astype(F32)
                return s + (lw_ref[cls] - slope * dist), vb_ref[hh, rows, :]

            m = jnp.full((tile, 1), NEG, F32)
            l = jnp.zeros((tile, 1), F32)
            acc = jnp.zeros((tile, HEAD_DIM), F32)
            for delta, cls in ((0, 0), (1, 1), (2, 2), (3, 2), (4, 3)):
                a, v_tile = scores(jnp.maximum(i - delta, 0), cls, delta)
                a = a + jnp.where(i >= delta, 0.0, NEG)
                m, l, acc = _online_softmax_step(a, v_tile, m, l, acc)

            def far(delta, st):
                a, v_tile = scores(i - delta, 4, delta)
                return _online_softmax_step(a, v_tile, *st)

            m, l, acc = lax.fori_loop(5, i + 1, far, (m, l, acc))
            outs.append(acc / l)
        o_ref[pl.ds(pl.multiple_of(i * tile, tile), tile), :] = jnp.concatenate(outs, axis=1).astype(BF16)
        return carry

    lax.fori_loop(0, n_tiles, q_tile, 0)


def _dilated(proj, q_gain, k_gain, *, batch, seq, col0):
    pairs = 4
    heads = 2 * pairs
    slopes = (2.0 ** (-8.0 * jnp.arange(1, heads + 1, dtype=F32) / heads)).astype(F32)

    def col(off):
        return pl.BlockSpec((seq, 2 * HEAD_DIM), lambda b, p: (b, col0 + off * pairs + p))

    return pl.pallas_call(
        functools.partial(_dilated_kernel, seq=seq),
        grid=(batch, pairs),
        in_specs=[pl.BlockSpec(memory_space=pltpu.SMEM),
                  col(0), col(1), col(2),
                  pl.BlockSpec((1, HEAD_DIM), lambda b, p: (0, 0)),
                  pl.BlockSpec((1, HEAD_DIM), lambda b, p: (0, 0)),
                  pl.BlockSpec((5, ATT_TILE, ATT_TILE), lambda b, p: (0, 0, 0))],
        out_specs=pl.BlockSpec((seq, 2 * HEAD_DIM), lambda b, p: (b, p)),
        out_shape=jax.ShapeDtypeStruct((batch * seq, heads * HEAD_DIM), BF16),
        scratch_shapes=[pltpu.VMEM((2, seq, HEAD_DIM), BF16)] * 3,
        compiler_params=_cparams("parallel", "parallel"),
        name="dilated",
    )(slopes, proj, proj, proj, q_gain.reshape(1, -1), k_gain.reshape(1, -1),
      jnp.asarray(_dilated_log_weights()))


def _nsa_overlap_t(n_blocks, n_cmp_rows):
    blk = np.arange(n_blocks)[:, None]
    start = np.arange(n_cmp_rows)[None, :] * CMP_STRIDE
    return ((start < (blk + 1) * SLC_BLOCK) & (start + CMP_LEN > blk * SLC_BLOCK)).astype(np.float32)


def _nsa_kernel(slopes_ref, x_ref, qg_ref, kg_ref, pe_ref, w1_ref, w2_ref, ovl_ref, o_ref,
                qn_ref, ks_ref, vs_ref, kw_ref, vw_ref, kc_ref, vc_ref, *, seq):
    tile = ATT_TILE
    n_tiles = seq // tile
    n_cmp = seq // CMP_STRIDE
    n_blk = seq // SLC_BLOCK
    grp = pl.program_id(1)
    slopes = [slopes_ref[grp * C_GROUP + j] for j in range(C_GROUP)]

    def prep(i, carry):
        rows = pl.ds(pl.multiple_of(i * tile, tile), tile)
        for j in range(C_GROUP):
            qj = _rms(x_ref[rows, j * HEAD_DIM:(j + 1) * HEAD_DIM], qg_ref[...]) * ATTN_SCALE
            qn_ref[pl.ds(pl.multiple_of((i * C_GROUP + j) * tile, tile), tile), :] = qj.astype(BF16)
        kc_ref[rows, :] = x_ref[rows, 256:320]
        vc_ref[rows, :] = x_ref[rows, 320:384]
        ks_ref[rows, :] = _rms(x_ref[rows, 384:448], kg_ref[1:2]).astype(BF16)
        vs_ref[rows, :] = x_ref[rows, 448:512].astype(BF16)
        kw_ref[rows, :] = _rms(x_ref[rows, 512:576], kg_ref[2:3]).astype(BF16)
        vw_ref[rows, :] = x_ref[rows, 576:640].astype(BF16)
        return carry

    lax.fori_loop(0, n_tiles, prep, 0)
    kc_ref[seq:, :] = jnp.zeros((CMP_LEN, HEAD_DIM), F32)
    vc_ref[seq:, :] = jnp.zeros((CMP_LEN, HEAD_DIM), F32)

    def compress(src_ref, which):
        hid = jnp.zeros((n_cmp, CMP_HIDDEN), F32)
        for pos in range(CMP_LEN):
            tok = src_ref[pl.ds(pos, n_cmp, stride=CMP_STRIDE), :] + pe_ref[which, pos:pos + 1, :]
            hid = hid + _dot(tok.astype(BF16), w1_ref[which, pos * HEAD_DIM:(pos + 1) * HEAD_DIM, :])
        return _dot(_silu(hid).astype(BF16), w2_ref[which])

    k_cmp = _rms(compress(kc_ref, 0), kg_ref[0:1]).astype(BF16)
    v_cmp = compress(vc_ref, 1).astype(BF16)
    ovl = ovl_ref[...].astype(BF16)

    rc_i = (lax.broadcasted_iota(jnp.int32, (tile, tile), 0)
            - lax.broadcasted_iota(jnp.int32, (tile, tile), 1))
    rc = rc_i.astype(F32)
    row_i = lax.broadcasted_iota(jnp.int32, (tile, tile), 0)
    lane_i = lax.broadcasted_iota(jnp.int32, (tile, tile), 1)
    cmp_end = (lax.broadcasted_iota(jnp.int32, (tile, n_cmp), 1) * CMP_STRIDE + (CMP_LEN - 1))
    blk_i = lax.broadcasted_iota(jnp.int32, (n_blk, tile), 0)
    tok_lane = lax.broadcasted_iota(jnp.int32, (n_blk, tile), 1)

    def q_tile(i, carry):
        t0 = i * tile
        rows = pl.ds(pl.multiple_of(t0, tile), tile)
        q4 = qn_ref[pl.ds(pl.multiple_of(i * C_GROUP * tile, tile), C_GROUP * tile), :]

        dist_c = (t0 + lax.broadcasted_iota(jnp.int32, (tile, n_cmp), 0) - cmp_end).astype(F32)
        vis_c = dist_c >= 0
        s_c = _dot_nt(q4, k_cmp)
        o_cmp = []
        p_sum = jnp.zeros((tile, n_cmp), F32)
        for j in range(C_GROUP):
            a = jnp.where(vis_c, s_c[j * tile:(j + 1) * tile] - slopes[j] * dist_c, NEG)
            m = jnp.max(a, axis=-1, keepdims=True)
            e = jnp.where(vis_c, jnp.exp(a - m), 0.0)
            l = jnp.sum(e, axis=-1, keepdims=True)
            p = e / jnp.where(l > 0, l, 1.0)
            o_cmp.append(_dot(p.astype(BF16), v_cmp))
            p_sum = p_sum + p
        hi, mid, lo = _split3(p_sum)
        imp = _dot_nt(ovl, hi) + _dot_nt(ovl, mid) + _dot_nt(ovl, lo)

        cur = lax.shift_right_logical(t0 + tok_lane, 6)
        forced = (blk_i == 0) | (blk_i == cur) | (blk_i == cur - 1)
        score = jnp.where(blk_i > cur, -1.0, jnp.where(forced, 1e30, imp))
        rank = jnp.zeros((n_blk, tile), F32)
        for other in range(n_blk):
            s_o = score[other:other + 1, :]
            rank = rank + jnp.where(s_o > score, 1.0,
                                    jnp.where(s_o == score, jnp.where(blk_i > other, 1.0, 0.0), 0.0))
        sel = jnp.where(rank < SLC_TOPN, jnp.where(score >= 0, 1.0, 0.0), 0.0)
        sel_t = jnp.concatenate([sel, jnp.zeros((tile - n_blk, tile), F32)], axis=0).T.astype(BF16)

        def slc_step(kt, st):
            krows = pl.ds(pl.multiple_of(kt * tile, tile), tile)
            expand = jnp.where(row_i == 2 * kt + lax.shift_right_logical(lane_i, 6), 1.0, 0.0).astype(BF16)
            picked = _dot(sel_t, expand)
            dist = rc + ((i - kt) * tile).astype(F32)
            mask = jnp.where(picked > 0.5, jnp.where(dist >= 0, 0.0, NEG), NEG)
            s = _dot_nt(q4, ks_ref[krows, :])
            v_tile = vs_ref[krows, :]
            new = []
            for j in range(C_GROUP):
                a = s[j * tile:(j + 1) * tile] + (mask - slopes[j] * dist)
                new.append(_online_softmax_step(a, v_tile, *st[j]))
            return tuple(new)

        init = tuple((jnp.full((tile, 1), NEG, F32), jnp.zeros((tile, 1), F32),
                      jnp.zeros((tile, HEAD_DIM), F32)) for _ in range(C_GROUP))
        slc = lax.fori_loop(0, i + 1, slc_step, init)

        win = init
        for delta in range(WIN // tile + 1):
            kt = jnp.maximum(i - delta, 0)
            krows = pl.ds(pl.multiple_of(kt * tile, tile), tile)
            dist = rc + float(delta * tile)
            if delta == 0:
                mask = jnp.where(rc >= 0, 0.0, NEG)
            elif delta == WIN // tile:
                mask = jnp.where(rc < 0, 0.0, NEG)
            else:
                mask = jnp.zeros((tile, tile), F32)
            mask = mask + jnp.where(i >= delta, 0.0, NEG)
            s = _dot_nt(q4, kw_ref[krows, :])
            v_tile = vw_ref[krows, :]
            win = tuple(_online_softmax_step(s[j * tile:(j + 1) * tile] + (mask - slopes[j] * dist),
                                             v_tile, *win[j]) for j in range(C_GROUP))

        gate = jax.nn.sigmoid(x_ref[rows, NSA_GATE_COL:NSA_GATE_COL + 128])
        outs = []
        for j in range(C_GROUP):
            o_slc = slc[j][2] / slc[j][1]
            o_win = win[j][2] / win[j][1]
            outs.append(gate[:, j:j + 1] * o_cmp[j]
                        + gate[:, C_GROUP + j:C_GROUP + j + 1] * o_slc
                        + gate[:, 2 * C_GROUP + j:2 * C_GROUP + j + 1] * o_win)
        o_ref[rows, :] = jnp.concatenate(outs, axis=1).astype(BF16)
        return carry

    lax.fori_loop(0, n_tiles, q_tile, 0)


def _nsa(proj, q_gain, k_gain, cmp_pe, cmp_w1, cmp_w2, *, batch, seq):
    heads = C_KV_HEADS * C_GROUP
    slopes = (2.0 ** (-8.0 * jnp.arange(1, heads + 1, dtype=F32) / heads)).astype(F32)
    n_cmp = seq // CMP_STRIDE
    n_blk = seq // SLC_BLOCK
    full = lambda *shape: pl.BlockSpec(shape, lambda b, g: (0,) * len(shape))
    kv = pltpu.VMEM((seq, HEAD_DIM), BF16)
    return pl.pallas_call(
        functools.partial(_nsa_kernel, seq=seq),
        grid=(batch, C_KV_HEADS),
        in_specs=[pl.BlockSpec(memory_space=pltpu.SMEM),
                  pl.BlockSpec((seq, NSA_GROUP_COLS), lambda b, g: (b, g)),
                  full(1, HEAD_DIM), full(3, HEAD_DIM), full(2, CMP_LEN, HEAD_DIM),
                  full(2, CMP_LEN * HEAD_DIM, CMP_HIDDEN), full(2, CMP_HIDDEN, HEAD_DIM),
                  full(n_blk, n_cmp)],
        out_specs=pl.BlockSpec((seq, C_GROUP * HEAD_DIM), lambda b, g: (b, g)),
        out_shape=jax.ShapeDtypeStruct((batch * seq, heads * HEAD_DIM), BF16),
        scratch_shapes=[pltpu.VMEM((seq * C_GROUP, HEAD_DIM), BF16), kv, kv, kv, kv,
                        pltpu.VMEM((seq + CMP_LEN, HEAD_DIM), F32),
                        pltpu.VMEM((seq + CMP_LEN, HEAD_DIM), F32)],
        compiler_params=_cparams("parallel", "parallel"),
        name="nsa",
    )(slopes, proj, q_gain.reshape(1, -1), k_gain, cmp_pe, cmp_w1.astype(BF16), cmp_w2.astype(BF16),
      jnp.asarray(_nsa_overlap_t(n_blk, n_cmp)))


def _nsa_in_weight(w_in, d_model):
    kv_w = C_KV_HEADS * HEAD_DIM
    cols = []
    for g in range(C_KV_HEADS):
        idx = list(range(g * C_GROUP * HEAD_DIM, (g + 1) * C_GROUP * HEAD_DIM))
        for part in range(6):
            base = d_model + part * kv_w + g * HEAD_DIM
            idx += list(range(base, base + HEAD_DIM))
        gate0 = d_model + 6 * kv_w
        idx += [gate0 + n * C_KV_HEADS * C_GROUP + g * C_GROUP + j for n in range(3) for j in range(C_GROUP)]
        cols.append(jnp.pad(w_in[:, np.asarray(idx)], ((0, 0), (0, NSA_GROUP_COLS - len(idx)))))
    return jnp.concatenate(cols, axis=1)


def kernel(x, attn_norm, ffn_norm, hy_w_in, hy_lb_logits, hy_o_gain, hy_q_gain, hy_k_gain, hy_w_out, nsa_w_in, nsa_q_gain, nsa_k_gain, nsa_cmp_pe, nsa_cmp_w1, nsa_cmp_w2, nsa_w_out, ffn_w_gate, ffn_w_up, ffn_w_down):
    batch, seq, d_model = x.shape
    depth = attn_norm.shape[0]
    a_width = hy_o_gain.shape[1]
    dff = ffn_w_gate.shape[2]
    h = x.reshape(batch * seq, d_model)
    for layer in range(depth):
        i = layer // 2
        if layer % 2 == 0:
            proj = _norm_matmul(h, attn_norm[layer], hy_w_in[i].astype(BF16), tm=1024, tn=512)
            a_out = _hgrn2(proj, hy_lb_logits, hy_o_gain[i], layer_idx=i, batch=batch, seq=seq)
            b_out = _dilated(proj, hy_q_gain[i], hy_k_gain[i], batch=batch, seq=seq,
                             col0=4 * a_width // (2 * HEAD_DIM))
            w_out = hy_w_out[i].astype(BF16)
            h = _proj_residual(h, [a_out, b_out], [w_out[:a_width], w_out[a_width:]], tm=1024)
        else:
            proj = _norm_matmul(h, attn_norm[layer], _nsa_in_weight(nsa_w_in[i], d_model).astype(BF16),
                                tm=1024, tn=NSA_GROUP_COLS)
            o = _nsa(proj, nsa_q_gain[i], nsa_k_gain[i], nsa_cmp_pe[i], nsa_cmp_w1[i], nsa_cmp_w2[i],
                     batch=batch, seq=seq)
            h = _proj_residual(h, [o], [nsa_w_out[i].astype(BF16)], tm=1024)
        h = _ffn(h, ffn_norm[layer], ffn_w_gate[layer].astype(BF16), ffn_w_up[layer].astype(BF16),
                 ffn_w_down[layer].astype(BF16), tm=512, tf=dff // 2)
    return h.reshape(batch, seq, d_model)
```

```python
import functools

import numpy as np
import jax
import jax.numpy as jnp
from jax import lax
from jax.experimental import pallas as pl
from jax.experimental.pallas import tpu as pltpu

F32 = jnp.float32
BF16 = jnp.bfloat16

EPS = 1e-6
HEAD_DIM = 64
ATTN_SCALE = HEAD_DIM ** -0.5
NEG = -1e30

A_EXPAND = 128
A_CHUNK = 32

DILATED_PATTERNS = ((128, 1), (512, 4), (2048, 16))
ATT_TILE = 128

C_KV_HEADS = 4
C_GROUP = 4
CMP_LEN = 32
CMP_STRIDE = 16
CMP_HIDDEN = 2 * HEAD_DIM
SLC_BLOCK = 64
SLC_TOPN = 8
SLC_CHUNK = 512
WIN = 512
NSA_GROUP_COLS = 768
NSA_GATE_COL = 640

V7X_VMEM_LIMIT = 56 * 1024 * 1024


def _cparams(*sem):
    return pltpu.CompilerParams(dimension_semantics=sem, vmem_limit_bytes=V7X_VMEM_LIMIT)


def _rms(x, gain):
    return x * lax.rsqrt(jnp.mean(x * x, axis=-1, keepdims=True) + EPS) * gain


def _silu(x):
    return x * jax.nn.sigmoid(x)


def _split3(x):
    hi = x.astype(BF16)
    r1 = x - hi.astype(F32)
    mid = r1.astype(BF16)
    lo = (r1 - mid.astype(F32)).astype(BF16)
    return hi, mid, lo


def _dot(a, b):
    return jnp.dot(a, b, preferred_element_type=F32)


def _dot_nt(a, b):
    return lax.dot_general(a, b, (((1,), (1,)), ((), ())), preferred_element_type=F32)


def _dot_tn(a, b):
    return lax.dot_general(a, b, (((0,), (0,)), ((), ())), preferred_element_type=F32)


def _norm_matmul_kernel(x_ref, g_ref, w_ref, o_ref, h_ref):
    @pl.when(pl.program_id(1) == 0)
    def _():
        h_ref[...] = _rms(x_ref[...], g_ref[...]).astype(BF16)

    o_ref[...] = _dot(h_ref[...], w_ref[...])


def _norm_matmul(x, gain, w, *, tm, tn):
    t, d = x.shape
    n = w.shape[1]
    return pl.pallas_call(
        _norm_matmul_kernel,
        grid=(t // tm, n // tn),
        in_specs=[pl.BlockSpec((tm, d), lambda i, j: (i, 0)),
                  pl.BlockSpec((1, d), lambda i, j: (0, 0)),
                  pl.BlockSpec((d, tn), lambda i, j: (0, j))],
        out_specs=pl.BlockSpec((tm, tn), lambda i, j: (i, j)),
        out_shape=jax.ShapeDtypeStruct((t, n), F32),
        scratch_shapes=[pltpu.VMEM((tm, d), BF16)],
        compiler_params=_cparams("parallel", "arbitrary"),
        name="norm_matmul",
    )(x, gain.reshape(1, d), w)


def _proj_residual_kernel(*refs, n_in):
    x_ref, o_ref = refs[0], refs[-1]
    acc = x_ref[...]
    for a_ref, w_ref in zip(refs[1:1 + n_in], refs[1 + n_in:1 + 2 * n_in]):
        acc = acc + _dot(a_ref[...], w_ref[...])
    o_ref[...] = acc


def _proj_residual(x, parts, weights, *, tm):
    t, d = x.shape
    n_in = len(parts)
    in_specs = [pl.BlockSpec((tm, d), lambda i: (i, 0))]
    in_specs += [pl.BlockSpec((tm, p.shape[1]), lambda i: (i, 0)) for p in parts]
    in_specs += [pl.BlockSpec(w.shape, lambda i: (0, 0)) for w in weights]
    return pl.pallas_call(
        functools.partial(_proj_residual_kernel, n_in=n_in),
        grid=(t // tm,),
        in_specs=in_specs,
        out_specs=pl.BlockSpec((tm, d), lambda i: (i, 0)),
        out_shape=jax.ShapeDtypeStruct((t, d), F32),
        compiler_params=_cparams("parallel"),
        name="proj_residual",
    )(x, *parts, *weights)


def _ffn_kernel(x_ref, g_ref, wg_ref, wu_ref, wd_ref, o_ref, h_ref, acc_ref):
    f = pl.program_id(1)

    @pl.when(f == 0)
    def _():
        x = x_ref[...]
        h_ref[...] = _rms(x, g_ref[...]).astype(BF16)
        acc_ref[...] = x

    h = h_ref[...]
    act = _silu(_dot(h, wg_ref[...])) * _dot(h, wu_ref[...])
    acc_ref[...] += _dot(act.astype(BF16), wd_ref[...])

    @pl.when(f == pl.num_programs(1) - 1)
    def _():
        o_ref[...] = acc_ref[...]


def _ffn(x, gain, w_gate, w_up, w_down, *, tm, tf):
    t, d = x.shape
    dff = w_gate.shape[1]
    return pl.pallas_call(
        _ffn_kernel,
        grid=(t // tm, dff // tf),
        in_specs=[pl.BlockSpec((tm, d), lambda i, f: (i, 0)),
                  pl.BlockSpec((1, d), lambda i, f: (0, 0)),
                  pl.BlockSpec((d, tf), lambda i, f: (0, f)),
                  pl.BlockSpec((d, tf), lambda i, f: (0, f)),
                  pl.BlockSpec((tf, d), lambda i, f: (f, 0))],
        out_specs=pl.BlockSpec((tm, d), lambda i, f: (i, 0)),
        out_shape=jax.ShapeDtypeStruct((t, d), F32),
        scratch_shapes=[pltpu.VMEM((tm, d), BF16), pltpu.VMEM((tm, d), F32)],
        compiler_params=_cparams("parallel", "arbitrary"),
        name="ffn",
    )(x, gain.reshape(1, d), w_gate, w_up, w_down)


def _hgrn2_kernel(q_ref, f_ref, v_ref, g_ref, lbl_ref, og_ref, o_ref, state_ref, p_ref, *,
                  layer_idx, seq, heads):
    c_len, hd = A_CHUNK, A_EXPAND
    logits = lbl_ref[...]
    e = jnp.exp(logits - jnp.max(logits, axis=0, keepdims=True))
    prob = e / jnp.sum(e, axis=0, keepdims=True)
    lb_all = jnp.sum(prob[:layer_idx + 1], axis=0, keepdims=True) - prob[0:1]
    og_all = og_ref[...]

    row = lax.broadcasted_iota(jnp.int32, (c_len, hd), 0)
    lane = lax.broadcasted_iota(jnp.int32, (c_len, hd), 1)
    tril = (lax.broadcasted_iota(jnp.int32, (c_len, c_len), 0)
            >= lax.broadcasted_iota(jnp.int32, (c_len, c_len), 1)).astype(BF16)
    ones = jnp.ones((hd, hd), BF16)

    state_ref[...] = jnp.zeros_like(state_ref)
    p_ref[...] = jnp.zeros_like(p_ref)

    def chunk(c, carry):
        rows = pl.ds(pl.multiple_of(c * c_len, c_len), c_len)
        for h in range(heads):
            cols = slice(h * hd, (h + 1) * hd)
            lb = lb_all[:, cols]
            f = lb + (1.0 - lb) * jax.nn.sigmoid(f_ref[rows, cols])
            logf = jnp.log(f)
            kk = 1.0 - f
            qq = _silu(q_ref[rows, cols])
            v = v_ref[rows, cols]
            hi, mid, lo = _split3(logf)
            b = _dot(tril, hi) + _dot(tril, mid) + _dot(tril, lo)
            for s in range(c_len):
                t0 = (s // 8) * 8
                d = b[t0:] - b[s:s + 1]
                e_s = jnp.exp(jnp.where(row[t0:] >= s, d, NEG))
                p_ref[h, s * c_len + t0:(s + 1) * c_len, :] = qq[t0:] * e_s * kk[s:s + 1]
            r = _dot(p_ref[h].astype(BF16), ones)
            scores = jnp.zeros((c_len, hd), F32)
            for s in range(c_len):
                scores = jnp.where(lane == s, r[s * c_len:(s + 1) * c_len], scores)
            o = _dot(scores[:, :c_len].astype(BF16), v.astype(BF16))
            st = state_ref[h]
            o = o + _dot_nt((qq * jnp.exp(b)).astype(BF16), st.astype(BF16))
            b_last = b[c_len - 1:c_len]
            k_dec = (kk * jnp.exp(b_last - b)).astype(BF16)
            state_ref[h] = st * jnp.exp(b_last) + _dot_tn(v.astype(BF16), k_dec)
            o = o * lax.rsqrt(jnp.mean(o * o, axis=-1, keepdims=True) + EPS)
            o_ref[rows, cols] = (o * og_all[:, cols] * _silu(g_ref[rows, cols])).astype(BF16)
        return carry

    lax.fori_loop(0, seq // c_len, chunk, 0)


def _hgrn2(proj, lb_logits, o_gain, *, layer_idx, batch, seq, heads_per_step):
    heads = o_gain.shape[0] // A_EXPAND
    groups = heads // heads_per_step
    width = heads_per_step * A_EXPAND
    n_layers = lb_logits.shape[0]

    def col(off):
        return pl.BlockSpec((seq, width), lambda b, h: (b, h + off * groups))

    return pl.pallas_call(
        functools.partial(_hgrn2_kernel, layer_idx=layer_idx, seq=seq, heads=heads_per_step),
        grid=(batch, groups),
        in_specs=[col(0), col(1), col(2), col(3),
                  pl.BlockSpec((n_layers, width), lambda b, h: (0, h)),
                  pl.BlockSpec((1, width), lambda b, h: (0, h))],
        out_specs=pl.BlockSpec((seq, width), lambda b, h: (b, h)),
        out_shape=jax.ShapeDtypeStruct((batch * seq, heads * A_EXPAND), BF16),
        scratch_shapes=[pltpu.VMEM((heads_per_step, A_EXPAND, A_EXPAND), F32),
                        pltpu.VMEM((heads_per_step, A_CHUNK * A_CHUNK, A_EXPAND), F32)],
        compiler_params=_cparams("parallel", "parallel"),
        name="hgrn2",
    )(proj, proj, proj, proj, lb_logits, o_gain.reshape(1, -1))


DIL_NEAR = DILATED_PATTERNS[:2]
DIL_FAR = DILATED_PATTERNS[2]
DIL_BACK = max(w for w, _ in DIL_NEAR)


def _dilated_near_log_weights():
    r = np.arange(ATT_TILE)[:, None]
    c = np.arange(DIL_BACK + ATT_TILE)[None, :]
    d = r + DIL_BACK - c
    mult = np.zeros_like(d)
    for window, dilation in DIL_NEAR:
        mult += (d >= 0) & (d % dilation == 0) & (d <= window)
    return np.where(mult > 0, np.log(np.maximum(mult, 1)), NEG).astype(np.float32), d.astype(np.float32)


def _online_softmax_step(a, v_tile, m, l, acc):
    m_new = jnp.maximum(m, jnp.max(a, axis=-1, keepdims=True))
    alpha = jnp.exp(m - m_new)
    p = jnp.exp(a - m_new)
    l = alpha * l + jnp.sum(p, axis=-1, keepdims=True)
    acc = alpha * acc + _dot(p.astype(BF16), v_tile)
    return m_new, l, acc


FAR_STAT = (2 * HEAD_DIM - HEAD_DIM) // 2


def _tile_rows(first_tile, n_tiles):
    start = first_tile * ATT_TILE
    if not isinstance(first_tile, int):
        start = pl.multiple_of(start, ATT_TILE)
    return pl.ds(start, n_tiles * ATT_TILE)


def _softmax_parts(a, v_tile):
    m = jnp.max(a, axis=-1, keepdims=True)
    p = jnp.exp(a - m)
    return m, jnp.sum(p, axis=-1, keepdims=True), _dot(p.astype(BF16), v_tile)


def _dilated_kernel(slopes_ref, q_ref, k_ref, v_ref, qg_ref, kg_ref, lw_ref, dist_ref, o_ref,
                    qn_ref, kn_ref, vb_ref, qr_ref, kr_ref, vr_ref, n32_ref, far_ref, nb_ref, *, seq):
    tile = ATT_TILE
    pair = pl.program_id(1)
    n_tiles = seq // tile
    back = DIL_BACK // tile
    stride = DIL_FAR[1]
    n_res = seq // stride
    rc = (lax.broadcasted_iota(jnp.int32, (n_res, n_res), 0)
          - lax.broadcasted_iota(jnp.int32, (n_res, n_res), 1)).astype(F32)
    far_mask = jnp.where(rc >= 0, 0.0, NEG)
    slopes = [slopes_ref[2 * pair + hh] for hh in range(2)]

    def copies(hh, values, nat_ref, res_ref):
        n32_ref[...] = values
        nat_ref[hh] = values.astype(BF16)
        for r in range(stride):
            res_ref[hh, r * n_res:(r + 1) * n_res, :] = n32_ref[pl.ds(r, n_res, stride=stride), :].astype(BF16)

    for hh in range(2):
        lanes = slice(hh * HEAD_DIM, (hh + 1) * HEAD_DIM)
        copies(hh, _rms(q_ref[:, lanes], qg_ref[...]) * ATTN_SCALE, qn_ref, qr_ref)
        copies(hh, _rms(k_ref[:, lanes], kg_ref[...]), kn_ref, kr_ref)
        copies(hh, v_ref[:, lanes], vb_ref, vr_ref)
        nb_ref[hh] = lw_ref[...] - slopes[hh] * dist_ref[...]

    def far_step(r, carry):
        rows = pl.ds(pl.multiple_of(r * n_res, n_res), n_res)
        for hh in range(2):
            a = _dot_nt(qr_ref[hh, rows, :], kr_ref[hh, rows, :]) + (far_mask - (slopes[hh] * stride) * rc)
            m, l, acc = _softmax_parts(a, vr_ref[hh, rows, :])
            far_ref[hh, pl.ds(r, n_res, stride=stride), :] = jnp.concatenate(
                [acc, jnp.broadcast_to(m, (n_res, FAR_STAT)), jnp.broadcast_to(l, (n_res, FAR_STAT))], axis=1)
        return carry

    lax.fori_loop(0, stride, far_step, 0)

    def near_step(i, first_tile, n_win):
        rows = _tile_rows(i, 1)
        krows = _tile_rows(first_tile, n_win)
        col0 = (back + 1 - n_win) * tile
        outs = []
        for hh in range(2):
            a = _dot_nt(qn_ref[hh, rows, :], kn_ref[hh, krows, :]) + nb_ref[hh, :, col0:]
            m1, l1, acc1 = _softmax_parts(a, vb_ref[hh, krows, :])
            far = far_ref[hh, rows, :]
            acc2 = far[:, :HEAD_DIM]
            m2 = far[:, HEAD_DIM:HEAD_DIM + 1]
            l2 = far[:, HEAD_DIM + FAR_STAT:HEAD_DIM + FAR_STAT + 1]
            m = jnp.maximum(m1, m2)
            w1, w2 = jnp.exp(m1 - m), jnp.exp(m2 - m)
            outs.append((w1 * acc1 + w2 * acc2) / (w1 * l1 + w2 * l2))
        o_ref[rows, :] = jnp.concatenate(outs, axis=1).astype(BF16)

    for i in range(min(back, n_tiles)):
        near_step(i, 0, i + 1)

    def near_body(i, carry):
        near_step(i, i - back, back + 1)
        return carry

    lax.fori_loop(back, n_tiles, near_body, 0)


def _dilated(proj, q_gain, k_gain, *, batch, seq, col0):
    pairs = 4
    heads = 2 * pairs
    slopes = (2.0 ** (-8.0 * jnp.arange(1, heads + 1, dtype=F32) / heads)).astype(F32)
    assert seq <= DIL_FAR[0] and seq % (DIL_FAR[1] * 8) == 0 and seq % ATT_TILE == 0
    log_w, dist = _dilated_near_log_weights()
    win = DIL_BACK + ATT_TILE

    def col(off):
        return pl.BlockSpec((seq, 2 * HEAD_DIM), lambda b, p: (b, col0 + off * pairs + p))

    head_copy = pltpu.VMEM((2, seq, HEAD_DIM), BF16)
    return pl.pallas_call(
        functools.partial(_dilated_kernel, seq=seq),
        grid=(batch, pairs),
        in_specs=[pl.BlockSpec(memory_space=pltpu.SMEM),
                  col(0), col(1), col(2),
                  pl.BlockSpec((1, HEAD_DIM), lambda b, p: (0, 0)),
                  pl.BlockSpec((1, HEAD_DIM), lambda b, p: (0, 0)),
                  pl.BlockSpec((ATT_TILE, win), lambda b, p: (0, 0)),
                  pl.BlockSpec((ATT_TILE, win), lambda b, p: (0, 0))],
        out_specs=pl.BlockSpec((seq, 2 * HEAD_DIM), lambda b, p: (b, p)),
        out_shape=jax.ShapeDtypeStruct((batch * seq, heads * HEAD_DIM), BF16),
        scratch_shapes=[head_copy] * 6 + [pltpu.VMEM((seq, HEAD_DIM), F32),
                                          pltpu.VMEM((2, seq, 2 * HEAD_DIM), F32),
                                          pltpu.VMEM((2, ATT_TILE, win), F32)],
        compiler_params=_cparams("parallel", "parallel"),
        name="dilated",
    )(slopes, proj, proj, proj, q_gain.reshape(1, -1), k_gain.reshape(1, -1),
      jnp.asarray(log_w), jnp.asarray(dist))


def _nsa_overlap_t(n_blocks, n_cmp_rows):
    blk = np.arange(n_blocks)[:, None]
    start = np.arange(n_cmp_rows)[None, :] * CMP_STRIDE
    return ((start < (blk + 1) * SLC_BLOCK) & (start + CMP_LEN > blk * SLC_BLOCK)).astype(np.float32)


def _nsa_kernel(slopes_ref, x_ref, qg_ref, kg_ref, pe_ref, w1_ref, w2_ref, ovl_ref, o_ref,
                qn_ref, ks_ref, vs_ref, kw_ref, vw_ref, kc_ref, vc_ref, rc_slc_ref, rc_win_ref, *, seq):
    tile = ATT_TILE
    n_tiles = seq // tile
    n_cmp = seq // CMP_STRIDE
    n_blk = seq // SLC_BLOCK
    grp = pl.program_id(1)
    slopes = [slopes_ref[grp * C_GROUP + j] for j in range(C_GROUP)]

    def prep(i, carry):
        rows = pl.ds(pl.multiple_of(i * tile, tile), tile)
        for j in range(C_GROUP):
            qj = _rms(x_ref[rows, j * HEAD_DIM:(j + 1) * HEAD_DIM], qg_ref[...]) * ATTN_SCALE
            qn_ref[pl.ds(pl.multiple_of((i * C_GROUP + j) * tile, tile), tile), :] = qj.astype(BF16)
        kc_ref[rows, :] = x_ref[rows, 256:320]
        vc_ref[rows, :] = x_ref[rows, 320:384]
        ks_ref[rows, :] = _rms(x_ref[rows, 384:448], kg_ref[1:2]).astype(BF16)
        vs_ref[rows, :] = x_ref[rows, 448:512].astype(BF16)
        kw_ref[rows, :] = _rms(x_ref[rows, 512:576], kg_ref[2:3]).astype(BF16)
        vw_ref[rows, :] = x_ref[rows, 576:640].astype(BF16)
        return carry

    lax.fori_loop(0, n_tiles, prep, 0)
    kc_ref[seq:, :] = jnp.zeros((CMP_LEN, HEAD_DIM), F32)
    vc_ref[seq:, :] = jnp.zeros((CMP_LEN, HEAD_DIM), F32)

    def compress(src_ref, which):
        hid = jnp.zeros((n_cmp, CMP_HIDDEN), F32)
        for pos in range(CMP_LEN):
            tok = src_ref[pl.ds(pos, n_cmp, stride=CMP_STRIDE), :] + pe_ref[which, pos:pos + 1, :]
            hid = hid + _dot(tok.astype(BF16), w1_ref[which, pos * HEAD_DIM:(pos + 1) * HEAD_DIM, :])
        return _dot(_silu(hid).astype(BF16), w2_ref[which])

    k_cmp = _rms(compress(kc_ref, 0), kg_ref[0:1]).astype(BF16)
    v_cmp = compress(vc_ref, 1).astype(BF16)
    ovl = ovl_ref[...].astype(BF16)

    rc_slc_ref[...] = (lax.broadcasted_iota(jnp.int32, (tile, SLC_CHUNK), 0)
                       - lax.broadcasted_iota(jnp.int32, (tile, SLC_CHUNK), 1)).astype(F32)
    rc_win_ref[...] = (lax.broadcasted_iota(jnp.int32, (tile, WIN + tile), 0)
                       - lax.broadcasted_iota(jnp.int32, (tile, WIN + tile), 1)).astype(F32)
    blk_row = lax.broadcasted_iota(jnp.int32, (tile, SLC_CHUNK), 0)
    blk_of_key = lax.shift_right_logical(lax.broadcasted_iota(jnp.int32, (tile, SLC_CHUNK), 1), 6)
    cmp_end = (lax.broadcasted_iota(jnp.int32, (tile, n_cmp), 1) * CMP_STRIDE + (CMP_LEN - 1))
    blk_i = lax.broadcasted_iota(jnp.int32, (n_blk, tile), 0)
    tok_lane = lax.broadcasted_iota(jnp.int32, (n_blk, tile), 1)

    def q_tile(i, carry):
        t0 = i * tile
        rows = pl.ds(pl.multiple_of(t0, tile), tile)
        q4 = qn_ref[pl.ds(pl.multiple_of(i * C_GROUP * tile, tile), C_GROUP * tile), :]

        dist_c = (t0 + lax.broadcasted_iota(jnp.int32, (tile, n_cmp), 0) - cmp_end).astype(F32)
        vis_c = dist_c >= 0
        s_c = _dot_nt(q4, k_cmp)
        o_cmp = []
        p_sum = jnp.zeros((tile, n_cmp), F32)
        for j in range(C_GROUP):
            a = jnp.where(vis_c, s_c[j * tile:(j + 1) * tile] - slopes[j] * dist_c, NEG)
            m = jnp.max(a, axis=-1, keepdims=True)
            e = jnp.where(vis_c, jnp.exp(a - m), 0.0)
            l = jnp.sum(e, axis=-1, keepdims=True)
            p = e / jnp.where(l > 0, l, 1.0)
            o_cmp.append(_dot(p.astype(BF16), v_cmp))
            p_sum = p_sum + p
        hi, mid, lo = _split3(p_sum)
        imp = _dot_nt(ovl, hi) + _dot_nt(ovl, mid) + _dot_nt(ovl, lo)

        cur = lax.shift_right_logical(t0 + tok_lane, 6)
        forced = (blk_i == 0) | (blk_i == cur) | (blk_i == cur - 1)
        score = jnp.where(blk_i > cur, -1.0, jnp.where(forced, 1e30, imp))
        rank = jnp.zeros((n_blk, tile), F32)
        for other in range(n_blk):
            s_o = score[other:other + 1, :]
            rank = rank + jnp.where(s_o > score, 1.0,
                                    jnp.where(s_o == score, jnp.where(blk_i > other, 1.0, 0.0), 0.0))
        sel = jnp.where(rank < SLC_TOPN, jnp.where(score >= 0, 1.0, 0.0), 0.0)
        sel_t = jnp.concatenate([sel, jnp.zeros((tile - n_blk, tile), F32)], axis=0).T.astype(BF16)

        def slc_step(c, st):
            krows = pl.ds(pl.multiple_of(c * SLC_CHUNK, SLC_CHUNK), SLC_CHUNK)
            expand = jnp.where(blk_row == c * (SLC_CHUNK // SLC_BLOCK) + blk_of_key, 1.0, 0.0).astype(BF16)
            picked = _dot(sel_t, expand)
            dist = rc_slc_ref[...] + (t0 - c * SLC_CHUNK).astype(F32)
            mask = jnp.where(picked > 0.5, jnp.where(dist >= 0, 0.0, NEG), NEG)
            s = _dot_nt(q4, ks_ref[krows, :])
            v_tile = vs_ref[krows, :]
            new = []
            for j in range(C_GROUP):
                a = s[j * tile:(j + 1) * tile] + (mask - slopes[j] * dist)
                new.append(_online_softmax_step(a, v_tile, *st[j]))
            return tuple(new)

        init = tuple((jnp.full((tile, 1), NEG, F32), jnp.zeros((tile, 1), F32),
                      jnp.zeros((tile, HEAD_DIM), F32)) for _ in range(C_GROUP))
        slc = lax.fori_loop(0, t0 // SLC_CHUNK + 1, slc_step, init)

        first = jnp.maximum(i - WIN // tile, 0)
        krows = pl.ds(pl.multiple_of(first * tile, tile), WIN + tile)
        dist = rc_win_ref[...] + ((i - first) * tile).astype(F32)
        mask = jnp.where(dist >= 0, jnp.where(dist < WIN, 0.0, NEG), NEG)
        s = _dot_nt(q4, kw_ref[krows, :])
        v_win = vw_ref[krows, :]
        win = [_softmax_parts(s[j * tile:(j + 1) * tile] + (mask - slopes[j] * dist), v_win)
               for j in range(C_GROUP)]

        gate = jax.nn.sigmoid(x_ref[rows, NSA_GATE_COL:NSA_GATE_COL + 128])
        outs = []
        for j in range(C_GROUP):
            o_slc = slc[j][2] / slc[j][1]
            o_win = win[j][2] / win[j][1]
            outs.append(gate[:, j:j + 1] * o_cmp[j]
                        + gate[:, C_GROUP + j:C_GROUP + j + 1] * o_slc
                        + gate[:, 2 * C_GROUP + j:2 * C_GROUP + j + 1] * o_win)
        o_ref[rows, :] = jnp.concatenate(outs, axis=1).astype(BF16)
        return carry

    lax.fori_loop(0, n_tiles, q_tile, 0)


def _nsa(proj, q_gain, k_gain, cmp_pe, cmp_w1, cmp_w2, *, batch, seq):
    heads = C_KV_HEADS * C_GROUP
    slopes = (2.0 ** (-8.0 * jnp.arange(1, heads + 1, dtype=F32) / heads)).astype(F32)
    n_cmp = seq // CMP_STRIDE
    n_blk = seq // SLC_BLOCK
    assert seq % SLC_CHUNK == 0 and seq >= WIN + ATT_TILE and n_blk <= ATT_TILE and n_cmp <= ATT_TILE
    full = lambda *shape: pl.BlockSpec(shape, lambda b, g: (0,) * len(shape))
    kv = pltpu.VMEM((seq, HEAD_DIM), BF16)
    return pl.pallas_call(
        functools.partial(_nsa_kernel, seq=seq),
        grid=(batch, C_KV_HEADS),
        in_specs=[pl.BlockSpec(memory_space=pltpu.SMEM),
                  pl.BlockSpec((seq, NSA_GROUP_COLS), lambda b, g: (b, g)),
                  full(1, HEAD_DIM), full(3, HEAD_DIM), full(2, CMP_LEN, HEAD_DIM),
                  full(2, CMP_LEN * HEAD_DIM, CMP_HIDDEN), full(2, CMP_HIDDEN, HEAD_DIM),
                  full(n_blk, n_cmp)],
        out_specs=pl.BlockSpec((seq, C_GROUP * HEAD_DIM), lambda b, g: (b, g)),
        out_shape=jax.ShapeDtypeStruct((batch * seq, heads * HEAD_DIM), BF16),
        scratch_shapes=[pltpu.VMEM((seq * C_GROUP, HEAD_DIM), BF16), kv, kv, kv, kv,
                        pltpu.VMEM((seq + CMP_LEN, HEAD_DIM), F32),
                        pltpu.VMEM((seq + CMP_LEN, HEAD_DIM), F32),
                        pltpu.VMEM((ATT_TILE, SLC_CHUNK), F32),
                        pltpu.VMEM((ATT_TILE, WIN + ATT_TILE), F32)],
        compiler_params=_cparams("parallel", "parallel"),
        name="nsa",
    )(slopes, proj, q_gain.reshape(1, -1), k_gain, cmp_pe, cmp_w1.astype(BF16), cmp_w2.astype(BF16),
      jnp.asarray(_nsa_overlap_t(n_blk, n_cmp)))


def _nsa_in_weight(w_in, d_model):
    kv_w = C_KV_HEADS * HEAD_DIM
    cols = []
    for g in range(C_KV_HEADS):
        idx = list(range(g * C_GROUP * HEAD_DIM, (g + 1) * C_GROUP * HEAD_DIM))
        for part in range(6):
            base = d_model + part * kv_w + g * HEAD_DIM
            idx += list(range(base, base + HEAD_DIM))
        gate0 = d_model + 6 * kv_w
        idx += [gate0 + n * C_KV_HEADS * C_GROUP + g * C_GROUP + j for n in range(3) for j in range(C_GROUP)]
        cols.append(jnp.pad(w_in[:, np.asarray(idx)], ((0, 0), (0, NSA_GROUP_COLS - len(idx)))))
    return jnp.concatenate(cols, axis=1)


def kernel(x, attn_norm, ffn_norm, hy_w_in, hy_lb_logits, hy_o_gain, hy_q_gain, hy_k_gain, hy_w_out, nsa_w_in, nsa_q_gain, nsa_k_gain, nsa_cmp_pe, nsa_cmp_w1, nsa_cmp_w2, nsa_w_out, ffn_w_gate, ffn_w_up, ffn_w_down):
    batch, seq, d_model = x.shape
    depth = attn_norm.shape[0]
    a_width = hy_o_gain.shape[1]
    dff = ffn_w_gate.shape[2]
    h = x.reshape(batch * seq, d_model)
    for layer in range(depth):
        i = layer // 2
        if layer % 2 == 0:
            proj = _norm_matmul(h, attn_norm[layer], hy_w_in[i].astype(BF16), tm=1024, tn=512)
            a_out = _hgrn2(proj, hy_lb_logits, hy_o_gain[i], layer_idx=i, batch=batch, seq=seq,
                           heads_per_step=2)
            b_out = _dilated(proj, hy_q_gain[i], hy_k_gain[i], batch=batch, seq=seq,
                             col0=4 * a_width // (2 * HEAD_DIM))
            w_out = hy_w_out[i].astype(BF16)
            h = _proj_residual(h, [a_out, b_out], [w_out[:a_width], w_out[a_width:]], tm=1024)
        else:
            proj = _norm_matmul(h, attn_norm[layer], _nsa_in_weight(nsa_w_in[i], d_model).astype(BF16),
                                tm=1024, tn=NSA_GROUP_COLS)
            o = _nsa(proj, nsa_q_gain[i], nsa_k_gain[i], nsa_cmp_pe[i], nsa_cmp_w1[i], nsa_cmp_w2[i],
                     batch=batch, seq=seq)
            h = _proj_residual(h, [o], [nsa_w_out[i].astype(BF16)], tm=1024)
        h = _ffn(h, ffn_norm[layer], ffn_w_gate[layer].astype(BF16), ffn_w_up[layer].astype(BF16),
                 ffn_w_down[layer].astype(BF16), tm=512, tf=dff // 2)
    return h.reshape(batch, seq, d_model)
```

```python
import functools

import numpy as np
import jax
import jax.numpy as jnp
from jax import lax
from jax.experimental import pallas as pl
from jax.experimental.pallas import tpu as pltpu

F32 = jnp.float32
BF16 = jnp.bfloat16

EPS = 1e-6
HEAD_DIM = 64
ATTN_SCALE = HEAD_DIM ** -0.5
NEG = -1e30

A_EXPAND = 128
A_CHUNK = 32
A_SUB = 8

DILATED_PATTERNS = ((128, 1), (512, 4), (2048, 16))
ATT_TILE = 128

C_KV_HEADS = 4
C_GROUP = 4
CMP_LEN = 32
CMP_STRIDE = 16
CMP_HIDDEN = 2 * HEAD_DIM
SLC_BLOCK = 64
SLC_TOPN = 8
SLC_CHUNK = 512
WIN = 512
NSA_GROUP_COLS = 768
NSA_GATE_COL = 640

V7X_VMEM_LIMIT = 56 * 1024 * 1024


def _cparams(*sem):
    return pltpu.CompilerParams(dimension_semantics=sem, vmem_limit_bytes=V7X_VMEM_LIMIT)


def _rms(x, gain):
    return x * lax.rsqrt(jnp.mean(x * x, axis=-1, keepdims=True) + EPS) * gain


def _silu(x):
    return x * jax.nn.sigmoid(x)


def _split3(x):
    hi = x.astype(BF16)
    r1 = x - hi.astype(F32)
    mid = r1.astype(BF16)
    lo = (r1 - mid.astype(F32)).astype(BF16)
    return hi, mid, lo


def _dot(a, b):
    return jnp.dot(a, b, preferred_element_type=F32)


def _dot_nt(a, b):
    return lax.dot_general(a, b, (((1,), (1,)), ((), ())), preferred_element_type=F32)


def _dot_tn(a, b):
    return lax.dot_general(a, b, (((0,), (0,)), ((), ())), preferred_element_type=F32)


def _norm_matmul_kernel(x_ref, g_ref, w_ref, o_ref, h_ref):
    @pl.when(pl.program_id(1) == 0)
    def _():
        h_ref[...] = _rms(x_ref[...], g_ref[...]).astype(BF16)

    o_ref[...] = _dot(h_ref[...], w_ref[...])


def _norm_matmul(x, gain, w, *, tm, tn):
    t, d = x.shape
    n = w.shape[1]
    return pl.pallas_call(
        _norm_matmul_kernel,
        grid=(t // tm, n // tn),
        in_specs=[pl.BlockSpec((tm, d), lambda i, j: (i, 0)),
                  pl.BlockSpec((1, d), lambda i, j: (0, 0)),
                  pl.BlockSpec((d, tn), lambda i, j: (0, j))],
        out_specs=pl.BlockSpec((tm, tn), lambda i, j: (i, j)),
        out_shape=jax.ShapeDtypeStruct((t, n), F32),
        scratch_shapes=[pltpu.VMEM((tm, d), BF16)],
        compiler_params=_cparams("parallel", "arbitrary"),
        name="norm_matmul",
    )(x, gain.reshape(1, d), w)


def _proj_residual_kernel(*refs, n_in):
    x_ref, o_ref = refs[0], refs[-1]
    acc = x_ref[...]
    for a_ref, w_ref in zip(refs[1:1 + n_in], refs[1 + n_in:1 + 2 * n_in]):
        acc = acc + _dot(a_ref[...], w_ref[...])
    o_ref[...] = acc


def _proj_residual(x, parts, weights, *, tm):
    t, d = x.shape
    n_in = len(parts)
    in_specs = [pl.BlockSpec((tm, d), lambda i: (i, 0))]
    in_specs += [pl.BlockSpec((tm, p.shape[1]), lambda i: (i, 0)) for p in parts]
    in_specs += [pl.BlockSpec(w.shape, lambda i: (0, 0)) for w in weights]
    return pl.pallas_call(
        functools.partial(_proj_residual_kernel, n_in=n_in),
        grid=(t // tm,),
        in_specs=in_specs,
        out_specs=pl.BlockSpec((tm, d), lambda i: (i, 0)),
        out_shape=jax.ShapeDtypeStruct((t, d), F32),
        compiler_params=_cparams("parallel"),
        name="proj_residual",
    )(x, *parts, *weights)


def _ffn_kernel(x_ref, g_ref, wg_ref, wu_ref, wd_ref, o_ref, h_ref, acc_ref):
    f = pl.program_id(1)

    @pl.when(f == 0)
    def _():
        x = x_ref[...]
        h_ref[...] = _rms(x, g_ref[...]).astype(BF16)
        acc_ref[...] = x

    h = h_ref[...]
    act = _silu(_dot(h, wg_ref[...])) * _dot(h, wu_ref[...])
    acc_ref[...] += _dot(act.astype(BF16), wd_ref[...])

    @pl.when(f == pl.num_programs(1) - 1)
    def _():
        o_ref[...] = acc_ref[...]


def _ffn(x, gain, w_gate, w_up, w_down, *, tm, tf):
    t, d = x.shape
    dff = w_gate.shape[1]
    return pl.pallas_call(
        _ffn_kernel,
        grid=(t // tm, dff // tf),
        in_specs=[pl.BlockSpec((tm, d), lambda i, f: (i, 0)),
                  pl.BlockSpec((1, d), lambda i, f: (0, 0)),
                  pl.BlockSpec((d, tf), lambda i, f: (0, f)),
                  pl.BlockSpec((d, tf), lambda i, f: (0, f)),
                  pl.BlockSpec((tf, d), lambda i, f: (f, 0))],
        out_specs=pl.BlockSpec((tm, d), lambda i, f: (i, 0)),
        out_shape=jax.ShapeDtypeStruct((t, d), F32),
        scratch_shapes=[pltpu.VMEM((tm, d), BF16), pltpu.VMEM((tm, d), F32)],
        compiler_params=_cparams("parallel", "arbitrary"),
        name="ffn",
    )(x, gain.reshape(1, d), w_gate, w_up, w_down)


def _hgrn2_kernel(q_ref, f_ref, v_ref, g_ref, lbl_ref, og_ref, o_ref, state_ref, *,
                  layer_idx, seq, heads, unroll):
    c_len, hd = A_CHUNK, A_EXPAND
    logits = lbl_ref[...]
    e = jnp.exp(logits - jnp.max(logits, axis=0, keepdims=True))
    prob = e / jnp.sum(e, axis=0, keepdims=True)
    lb_all = jnp.sum(prob[:layer_idx + 1], axis=0, keepdims=True) - prob[0:1]
    og_all = og_ref[...]

    sub, n_sub = A_SUB, c_len // A_SUB
    row = lax.broadcasted_iota(jnp.int32, (c_len, hd), 0)
    row_sub = lax.broadcasted_iota(jnp.int32, (sub, hd), 0)
    lane_sub = lax.broadcasted_iota(jnp.int32, (sub, c_len), 1)
    tril = (lax.broadcasted_iota(jnp.int32, (c_len, c_len), 0)
            >= lax.broadcasted_iota(jnp.int32, (c_len, c_len), 1)).astype(BF16)
    ones = jnp.ones((hd, hd), BF16)

    state_ref[...] = jnp.zeros_like(state_ref)

    def step(c, carry):
        chains = [(h, u) for h in range(heads) for u in range(unroll)]
        rows = {u: pl.ds(pl.multiple_of((c * unroll + u) * c_len, c_len), c_len) for u in range(unroll)}
        cols = {h: slice(h * hd, (h + 1) * hd) for h in range(heads)}

        qq, kk, v, b = {}, {}, {}, {}
        for ch in chains:
            h, u = ch
            lb = lb_all[:, cols[h]]
            f = lb + (1.0 - lb) * jax.nn.sigmoid(f_ref[rows[u], cols[h]])
            kk[ch] = 1.0 - f
            qq[ch] = _silu(q_ref[rows[u], cols[h]])
            v[ch] = v_ref[rows[u], cols[h]].astype(BF16)
            hi, mid, lo = _split3(jnp.log(f))
            b[ch] = _dot(tril, hi) + _dot(tril, mid) + _dot(tril, lo)

        r = {}
        for ch in chains:
            pieces = []
            for i in range(n_sub):
                lo_ = i * sub
                for s in range(sub):
                    d = b[ch][lo_:lo_ + sub] - b[ch][lo_ + s:lo_ + s + 1]
                    e_s = jnp.exp(jnp.where(row_sub >= s, d, NEG))
                    pieces.append(qq[ch][lo_:lo_ + sub] * e_s * kk[ch][lo_ + s:lo_ + s + 1])
            r[ch] = _dot(jnp.concatenate(pieces, axis=0).astype(BF16), ones)[:, :c_len]

        o = {}
        for ch in chains:
            blocks = []
            for i in range(n_sub):
                lo_ = i * sub
                sc = jnp.zeros((sub, c_len), F32)
                for s in range(sub):
                    at = (lo_ + s) * sub
                    sc = jnp.where(lane_sub == lo_ + s, r[ch][at:at + sub], sc)
                if i > 0:
                    b_ref_row = b[ch][lo_ - 1:lo_]
                    q_dec = (qq[ch][lo_:lo_ + sub] * jnp.exp(b[ch][lo_:lo_ + sub] - b_ref_row)).astype(BF16)
                    k_grow = (kk[ch] * jnp.exp(jnp.where(row < lo_, b_ref_row - b[ch], NEG))).astype(BF16)
                    sc = sc + _dot_nt(q_dec, k_grow)
                blocks.append(sc)
            o[ch] = _dot(jnp.concatenate(blocks, axis=0).astype(BF16), v[ch])

        upd, q_all = {}, {}
        for ch in chains:
            b_last = b[ch][c_len - 1:c_len]
            k_dec = (kk[ch] * jnp.exp(b_last - b[ch])).astype(BF16)
            upd[ch] = (jnp.exp(b_last), _dot_tn(v[ch], k_dec))
            q_all[ch] = (qq[ch] * jnp.exp(b[ch])).astype(BF16)

        for h in range(heads):
            st = state_ref[h]
            for u in range(unroll):
                ch = (h, u)
                o_c = o[ch] + _dot_nt(q_all[ch], st.astype(BF16))
                st = st * upd[ch][0] + upd[ch][1]
                o_c = o_c * lax.rsqrt(jnp.mean(o_c * o_c, axis=-1, keepdims=True) + EPS)
                o_ref[rows[u], cols[h]] = (o_c * og_all[:, cols[h]] * _silu(g_ref[rows[u], cols[h]])).astype(BF16)
            state_ref[h] = st
        return carry

    lax.fori_loop(0, seq // (c_len * unroll), step, 0)


def _hgrn2(proj, lb_logits, o_gain, *, layer_idx, batch, seq, heads_per_step, chunks_per_step):
    heads = o_gain.shape[0] // A_EXPAND
    groups = heads // heads_per_step
    width = heads_per_step * A_EXPAND
    n_layers = lb_logits.shape[0]

    def col(off):
        return pl.BlockSpec((seq, width), lambda b, h: (b, h + off * groups))

    return pl.pallas_call(
        functools.partial(_hgrn2_kernel, layer_idx=layer_idx, seq=seq, heads=heads_per_step,
                          unroll=chunks_per_step),
        grid=(batch, groups),
        in_specs=[col(0), col(1), col(2), col(3),
                  pl.BlockSpec((n_layers, width), lambda b, h: (0, h)),
                  pl.BlockSpec((1, width), lambda b, h: (0, h))],
        out_specs=pl.BlockSpec((seq, width), lambda b, h: (b, h)),
        out_shape=jax.ShapeDtypeStruct((batch * seq, heads * A_EXPAND), BF16),
        scratch_shapes=[pltpu.VMEM((heads_per_step, A_EXPAND, A_EXPAND), F32)],
        compiler_params=_cparams("parallel", "parallel"),
        name="hgrn2",
    )(proj, proj, proj, proj, lb_logits, o_gain.reshape(1, -1))


DIL_NEAR = DILATED_PATTERNS[:2]
DIL_FAR = DILATED_PATTERNS[2]
DIL_BACK = max(w for w, _ in DIL_NEAR)


def _dilated_near_log_weights():
    r = np.arange(ATT_TILE)[:, None]
    c = np.arange(DIL_BACK + ATT_TILE)[None, :]
    d = r + DIL_BACK - c
    mult = np.zeros_like(d)
    for window, dilation in DIL_NEAR:
        mult += (d >= 0) & (d % dilation == 0) & (d <= window)
    return np.where(mult > 0, np.log(np.maximum(mult, 1)), NEG).astype(np.float32)


FAR_STAT = (2 * HEAD_DIM - HEAD_DIM) // 2
FAR_UNROLL = 2
NEAR_UNROLL = 2
POS_SHIFT = 7
POS_RADIX = 1 << POS_SHIFT


def _tile_rows(first_tile, n_tiles):
    start = first_tile * ATT_TILE
    if not isinstance(first_tile, int):
        start = pl.multiple_of(start, ATT_TILE)
    return pl.ds(start, n_tiles * ATT_TILE)


def _softmax_blocks(qs, ks, biases, vs):
    return _softmax_of([_dot_nt(q, k) + bias for q, k, bias in zip(qs, ks, biases)], vs)


def _softmax_of(a, vs):
    m = [jnp.max(x, axis=-1, keepdims=True) for x in a]
    p = [jnp.exp(x - mm) for x, mm in zip(a, m)]
    l = [jnp.sum(x, axis=-1, keepdims=True) for x in p]
    acc = [_dot(x.astype(BF16), v) for x, v in zip(p, vs)]
    return list(zip(m, l, acc))


def _online_softmax_of(a, v_tile, states):
    m_new = [jnp.maximum(st[0], jnp.max(x, axis=-1, keepdims=True)) for x, st in zip(a, states)]
    alpha = [jnp.exp(st[0] - mn) for st, mn in zip(states, m_new)]
    p = [jnp.exp(x - mn) for x, mn in zip(a, m_new)]
    l = [al * st[1] + jnp.sum(x, axis=-1, keepdims=True) for al, st, x in zip(alpha, states, p)]
    pv = [_dot(x.astype(BF16), v_tile) for x in p]
    return tuple((mn, ll, al * st[2] + o) for mn, ll, al, st, o in zip(m_new, l, alpha, states, pv))


def _slope_pieces(slopes):
    hi = slopes.astype(BF16).astype(F32)
    mid = (slopes - hi).astype(BF16).astype(F32)
    lo = (slopes - hi - mid).astype(BF16).astype(F32)
    return jnp.stack([hi, mid, lo], axis=1).reshape(-1)


def _alibi_query_cols(pieces, rows):
    lane = lax.broadcasted_iota(jnp.int32, (rows, HEAD_DIM), 1)
    out = jnp.zeros((rows, HEAD_DIM), F32)
    for n, piece in enumerate(pieces):
        out = jnp.where(lane == n, piece * POS_RADIX, jnp.where(lane == n + 3, piece, out))
    return out


def _alibi_key_cols(pos):
    lane = lax.broadcasted_iota(jnp.int32, pos.shape, 1)
    hi = lax.shift_right_logical(pos, POS_SHIFT).astype(F32)
    lo = jnp.bitwise_and(pos, POS_RADIX - 1).astype(F32)
    return jnp.where(lane < 3, hi, jnp.where(lane < 6, lo, 0.0))


def _dilated_kernel(pieces_ref, q_ref, k_ref, v_ref, qg_ref, kg_ref, lw_ref, o_ref,
                    qn_ref, kn_ref, vb_ref, qr_ref, kr_ref, vr_ref, n32_ref, far_ref, *, seq):
    tile = ATT_TILE
    pair = pl.program_id(1)
    n_tiles = seq // tile
    back = DIL_BACK // tile
    stride = DIL_FAR[1]
    n_res = seq // stride
    far_mask = jnp.where(lax.broadcasted_iota(jnp.int32, (n_res, n_res), 0)
                         >= lax.broadcasted_iota(jnp.int32, (n_res, n_res), 1), 0.0, NEG)
    pos = lax.broadcasted_iota(jnp.int32, (seq, HEAD_DIM), 0)

    def copies(hh, values, nat_ref, res_ref):
        width = values.shape[1]
        n32_ref[:, :width] = values
        nat_ref[hh] = values.astype(BF16)
        for r in range(stride):
            res_ref[hh, r * n_res:(r + 1) * n_res, :] = (
                n32_ref[pl.ds(r, n_res, stride=stride), :width].astype(BF16))

    for hh in range(2):
        lanes = slice(hh * HEAD_DIM, (hh + 1) * HEAD_DIM)
        pieces = [pieces_ref[(2 * pair + hh) * 3 + n] for n in range(3)]
        q_aug = jnp.concatenate([_rms(q_ref[:, lanes], qg_ref[...]) * ATTN_SCALE,
                                 _alibi_query_cols(pieces, seq)], axis=1)
        k_aug = jnp.concatenate([_rms(k_ref[:, lanes], kg_ref[...]), _alibi_key_cols(pos)], axis=1)
        copies(hh, q_aug, qn_ref, qr_ref)
        copies(hh, k_aug, kn_ref, kr_ref)
        copies(hh, v_ref[:, lanes], vb_ref, vr_ref)

    def far_step(step, carry):
        res = [FAR_UNROLL * step + n for n in range(FAR_UNROLL)]
        units = [(hh, n) for n in range(FAR_UNROLL) for hh in range(2)]
        rows = [pl.ds(pl.multiple_of(r * n_res, n_res), n_res) for r in res]
        parts = _softmax_blocks([qr_ref[hh, rows[n], :] for hh, n in units],
                                [kr_ref[hh, rows[n], :] for hh, n in units],
                                [far_mask] * len(units),
                                [vr_ref[hh, rows[n], :] for hh, n in units])
        for (hh, n), (m, l, acc) in zip(units, parts):
            far_ref[hh, pl.ds(res[n], n_res, stride=stride), :] = jnp.concatenate(
                [acc, jnp.broadcast_to(m, (n_res, FAR_STAT)), jnp.broadcast_to(l, (n_res, FAR_STAT))], axis=1)
        return carry

    lax.fori_loop(0, stride // FAR_UNROLL, far_step, 0)

    def near_tiles(tiles):
        units = [(hh, t) for t in range(len(tiles)) for hh in range(2)]
        rows = [_tile_rows(i, 1) for i, _, _ in tiles]
        krows = [_tile_rows(first, n_win) for _, first, n_win in tiles]
        bias = [lw_ref[:, (back + 1 - n_win) * tile:] for _, _, n_win in tiles]
        parts = _softmax_blocks([qn_ref[hh, rows[t], :] for hh, t in units],
                                [kn_ref[hh, krows[t], :] for hh, t in units],
                                [bias[t] for hh, t in units],
                                [vb_ref[hh, krows[t], :] for hh, t in units])
        outs = {}
        for (hh, t), (m1, l1, acc1) in zip(units, parts):
            far = far_ref[hh, rows[t], :]
            acc2 = far[:, :HEAD_DIM]
            m2 = far[:, HEAD_DIM:HEAD_DIM + 1]
            l2 = far[:, HEAD_DIM + FAR_STAT:HEAD_DIM + FAR_STAT + 1]
            m = jnp.maximum(m1, m2)
            w1, w2 = jnp.exp(m1 - m), jnp.exp(m2 - m)
            outs[hh, t] = (w1 * acc1 + w2 * acc2) / (w1 * l1 + w2 * l2)
        for t in range(len(tiles)):
            o_ref[rows[t], :] = jnp.concatenate([outs[0, t], outs[1, t]], axis=1).astype(BF16)

    head_tiles = min(back, n_tiles)
    near_tiles([(i, 0, i + 1) for i in range(head_tiles)])

    def near_body(step, carry):
        first = head_tiles + NEAR_UNROLL * step
        near_tiles([(first + n, first + n - back, back + 1) for n in range(NEAR_UNROLL)])
        return carry

    lax.fori_loop(0, (n_tiles - head_tiles) // NEAR_UNROLL, near_body, 0)


def _dilated(proj, q_gain, k_gain, *, batch, seq, col0):
    pairs = 4
    heads = 2 * pairs
    slopes = (2.0 ** (-8.0 * jnp.arange(1, heads + 1, dtype=F32) / heads)).astype(F32)
    assert seq <= DIL_FAR[0] and seq % (DIL_FAR[1] * 8) == 0 and seq % ATT_TILE == 0
    assert seq <= POS_RADIX * 256 and DIL_FAR[1] % FAR_UNROLL == 0
    assert (seq // ATT_TILE - min(DIL_BACK // ATT_TILE, seq // ATT_TILE)) % NEAR_UNROLL == 0
    win = DIL_BACK + ATT_TILE

    def col(off):
        return pl.BlockSpec((seq, 2 * HEAD_DIM), lambda b, p: (b, col0 + off * pairs + p))

    qk_copy = pltpu.VMEM((2, seq, 2 * HEAD_DIM), BF16)
    v_copy = pltpu.VMEM((2, seq, HEAD_DIM), BF16)
    return pl.pallas_call(
        functools.partial(_dilated_kernel, seq=seq),
        grid=(batch, pairs),
        in_specs=[pl.BlockSpec(memory_space=pltpu.SMEM),
                  col(0), col(1), col(2),
                  pl.BlockSpec((1, HEAD_DIM), lambda b, p: (0, 0)),
                  pl.BlockSpec((1, HEAD_DIM), lambda b, p: (0, 0)),
                  pl.BlockSpec((ATT_TILE, win), lambda b, p: (0, 0))],
        out_specs=pl.BlockSpec((seq, 2 * HEAD_DIM), lambda b, p: (b, p)),
        out_shape=jax.ShapeDtypeStruct((batch * seq, heads * HEAD_DIM), BF16),
        scratch_shapes=[qk_copy, qk_copy, v_copy, qk_copy, qk_copy, v_copy,
                        pltpu.VMEM((seq, 2 * HEAD_DIM), F32),
                        pltpu.VMEM((2, seq, 2 * HEAD_DIM), F32)],
        compiler_params=_cparams("parallel", "parallel"),
        name="dilated",
    )(_slope_pieces(slopes), proj, proj, proj, q_gain.reshape(1, -1), k_gain.reshape(1, -1),
      jnp.asarray(_dilated_near_log_weights()))


def _nsa_overlap_t(n_blocks, n_cmp_rows):
    blk = np.arange(n_blocks)[:, None]
    start = np.arange(n_cmp_rows)[None, :] * CMP_STRIDE
    return ((start < (blk + 1) * SLC_BLOCK) & (start + CMP_LEN > blk * SLC_BLOCK)).astype(np.float32)


def _nsa_kernel(pieces_ref, x_ref, qg_ref, kg_ref, pe_ref, w1_ref, w2_ref, ovl_ref, o_ref,
                qn_ref, ks_ref, vs_ref, kw_ref, vw_ref, kc_ref, vc_ref, rc_slc_ref, rc_win_ref, *, seq):
    tile = ATT_TILE
    n_tiles = seq // tile
    n_cmp = seq // CMP_STRIDE
    n_blk = seq // SLC_BLOCK
    grp = pl.program_id(1)
    heads = range(C_GROUP)

    q_cols = [_alibi_query_cols([pieces_ref[(grp * C_GROUP + j) * 3 + n] for n in range(3)], tile)
              for j in heads]
    row_pos = lax.broadcasted_iota(jnp.int32, (tile, HEAD_DIM), 0)

    def prep(i, carry):
        rows = pl.ds(pl.multiple_of(i * tile, tile), tile)
        k_cols = _alibi_key_cols(i * tile + row_pos)
        for j in heads:
            qj = _rms(x_ref[rows, j * HEAD_DIM:(j + 1) * HEAD_DIM], qg_ref[...]) * ATTN_SCALE
            qn_ref[pl.ds(pl.multiple_of((i * C_GROUP + j) * tile, tile), tile), :] = (
                jnp.concatenate([qj, q_cols[j]], axis=1).astype(BF16))
        kc_ref[rows, :] = x_ref[rows, 256:320]
        vc_ref[rows, :] = x_ref[rows, 320:384]
        ks_ref[rows, :] = jnp.concatenate([_rms(x_ref[rows, 384:448], kg_ref[1:2]), k_cols], axis=1).astype(BF16)
        vs_ref[rows, :] = x_ref[rows, 448:512].astype(BF16)
        kw_ref[rows, :] = jnp.concatenate([_rms(x_ref[rows, 512:576], kg_ref[2:3]), k_cols], axis=1).astype(BF16)
        vw_ref[rows, :] = x_ref[rows, 576:640].astype(BF16)
        return carry

    lax.fori_loop(0, n_tiles, prep, 0)
    kc_ref[seq:, :] = jnp.zeros((CMP_LEN, HEAD_DIM), F32)
    vc_ref[seq:, :] = jnp.zeros((CMP_LEN, HEAD_DIM), F32)

    def compress(src_ref, which):
        hid = jnp.zeros((n_cmp, CMP_HIDDEN), F32)
        for pos in range(CMP_LEN):
            tok = src_ref[pl.ds(pos, n_cmp, stride=CMP_STRIDE), :] + pe_ref[which, pos:pos + 1, :]
            hid = hid + _dot(tok.astype(BF16), w1_ref[which, pos * HEAD_DIM:(pos + 1) * HEAD_DIM, :])
        return _dot(_silu(hid).astype(BF16), w2_ref[which])

    cmp_pos = lax.broadcasted_iota(jnp.int32, (n_cmp, HEAD_DIM), 0) * CMP_STRIDE + (CMP_LEN - 1)
    k_cmp = jnp.concatenate([_rms(compress(kc_ref, 0), kg_ref[0:1]), _alibi_key_cols(cmp_pos)],
                            axis=1).astype(BF16)
    v_cmp = compress(vc_ref, 1).astype(BF16)
    ovl = ovl_ref[...].astype(BF16)

    rc_slc_ref[...] = (lax.broadcasted_iota(jnp.int32, (tile, SLC_CHUNK), 0)
                       - lax.broadcasted_iota(jnp.int32, (tile, SLC_CHUNK), 1)).astype(F32)
    rc_win_ref[...] = (lax.broadcasted_iota(jnp.int32, (tile, WIN + tile), 0)
                       - lax.broadcasted_iota(jnp.int32, (tile, WIN + tile), 1)).astype(F32)
    blk_row = lax.broadcasted_iota(jnp.int32, (tile, SLC_CHUNK), 0)
    blk_of_key = lax.shift_right_logical(lax.broadcasted_iota(jnp.int32, (tile, SLC_CHUNK), 1), 6)
    cmp_end = (lax.broadcasted_iota(jnp.int32, (tile, n_cmp), 1) * CMP_STRIDE + (CMP_LEN - 1))
    blk_i = lax.broadcasted_iota(jnp.int32, (n_blk, tile), 0)
    tok_lane = lax.broadcasted_iota(jnp.int32, (n_blk, tile), 1)

    def head_rows(s):
        return [s[j * tile:(j + 1) * tile] for j in heads]

    def q_tile(i, carry):
        t0 = i * tile
        rows = pl.ds(pl.multiple_of(t0, tile), tile)
        q4 = qn_ref[pl.ds(pl.multiple_of(i * C_GROUP * tile, tile), C_GROUP * tile), :]

        vis_c = (t0 + lax.broadcasted_iota(jnp.int32, (tile, n_cmp), 0)) >= cmp_end
        a_c = [jnp.where(vis_c, x, NEG) for x in head_rows(_dot_nt(q4, k_cmp))]
        m_c = [jnp.max(x, axis=-1, keepdims=True) for x in a_c]
        e_c = [jnp.where(vis_c, jnp.exp(x - m), 0.0) for x, m in zip(a_c, m_c)]
        l_c = [jnp.sum(x, axis=-1, keepdims=True) for x in e_c]
        p_c = [x / jnp.where(l > 0, l, 1.0) for x, l in zip(e_c, l_c)]
        o_cmp = [_dot(x.astype(BF16), v_cmp) for x in p_c]
        hi, mid, lo = _split3(p_c[0] + p_c[1] + p_c[2] + p_c[3])
        imp = _dot_nt(ovl, hi) + _dot_nt(ovl, mid) + _dot_nt(ovl, lo)

        cur = lax.shift_right_logical(t0 + tok_lane, 6)
        forced = (blk_i == 0) | (blk_i == cur) | (blk_i == cur - 1)
        score = jnp.where(blk_i > cur, -1.0, jnp.where(forced, 1e30, imp))
        rank = jnp.zeros((n_blk, tile), F32)
        for other in range(n_blk):
            s_o = score[other:other + 1, :]
            rank = rank + jnp.where(s_o > score, 1.0,
                                    jnp.where(s_o == score, jnp.where(blk_i > other, 1.0, 0.0), 0.0))
        sel = jnp.where(rank < SLC_TOPN, jnp.where(score >= 0, 1.0, 0.0), 0.0)
        sel_t = jnp.concatenate([sel, jnp.zeros((tile - n_blk, tile), F32)], axis=0).T.astype(BF16)

        def slc_step(c, st):
            krows = pl.ds(pl.multiple_of(c * SLC_CHUNK, SLC_CHUNK), SLC_CHUNK)
            expand = jnp.where(blk_row == c * (SLC_CHUNK // SLC_BLOCK) + blk_of_key, 1.0, 0.0).astype(BF16)
            picked = _dot(sel_t, expand)
            dist = rc_slc_ref[...] + jnp.asarray(t0 - c * SLC_CHUNK, F32)
            mask = jnp.where(picked > 0.5, jnp.where(dist >= 0, 0.0, NEG), NEG)
            a = [x + mask for x in head_rows(_dot_nt(q4, ks_ref[krows, :]))]
            return _online_softmax_of(a, vs_ref[krows, :], st)

        init = tuple((jnp.full((tile, 1), NEG, F32), jnp.zeros((tile, 1), F32),
                      jnp.zeros((tile, HEAD_DIM), F32)) for _ in heads)
        slc = lax.fori_loop(0, t0 // SLC_CHUNK + 1, slc_step, init)

        first = jnp.maximum(i - WIN // tile, 0)
        krows = pl.ds(pl.multiple_of(first * tile, tile), WIN + tile)
        dist = rc_win_ref[...] + jnp.asarray((i - first) * tile, F32)
        mask = jnp.where(dist >= 0, jnp.where(dist < WIN, 0.0, NEG), NEG)
        win = _softmax_of([x + mask for x in head_rows(_dot_nt(q4, kw_ref[krows, :]))],
                          [vw_ref[krows, :]] * C_GROUP)

        gate = jax.nn.sigmoid(x_ref[rows, NSA_GATE_COL:NSA_GATE_COL + 128])
        outs = []
        for j in heads:
            o_slc = slc[j][2] / slc[j][1]
            o_win = win[j][2] / win[j][1]
            outs.append(gate[:, j:j + 1] * o_cmp[j]
                        + gate[:, C_GROUP + j:C_GROUP + j + 1] * o_slc
                        + gate[:, 2 * C_GROUP + j:2 * C_GROUP + j + 1] * o_win)
        o_ref[rows, :] = jnp.concatenate(outs, axis=1).astype(BF16)
        return carry

    lax.fori_loop(0, n_tiles, q_tile, 0)


def _nsa(proj, q_gain, k_gain, cmp_pe, cmp_w1, cmp_w2, *, batch, seq):
    heads = C_KV_HEADS * C_GROUP
    slopes = (2.0 ** (-8.0 * jnp.arange(1, heads + 1, dtype=F32) / heads)).astype(F32)
    n_cmp = seq // CMP_STRIDE
    n_blk = seq // SLC_BLOCK
    assert seq % SLC_CHUNK == 0 and seq >= WIN + ATT_TILE and n_blk <= ATT_TILE and n_cmp <= ATT_TILE
    full = lambda *shape: pl.BlockSpec(shape, lambda b, g: (0,) * len(shape))
    key_copy = pltpu.VMEM((seq, 2 * HEAD_DIM), BF16)
    val_copy = pltpu.VMEM((seq, HEAD_DIM), BF16)
    return pl.pallas_call(
        functools.partial(_nsa_kernel, seq=seq),
        grid=(batch, C_KV_HEADS),
        in_specs=[pl.BlockSpec(memory_space=pltpu.SMEM),
                  pl.BlockSpec((seq, NSA_GROUP_COLS), lambda b, g: (b, g)),
                  full(1, HEAD_DIM), full(3, HEAD_DIM), full(2, CMP_LEN, HEAD_DIM),
                  full(2, CMP_LEN * HEAD_DIM, CMP_HIDDEN), full(2, CMP_HIDDEN, HEAD_DIM),
                  full(n_blk, n_cmp)],
        out_specs=pl.BlockSpec((seq, C_GROUP * HEAD_DIM), lambda b, g: (b, g)),
        out_shape=jax.ShapeDtypeStruct((batch * seq, heads * HEAD_DIM), BF16),
        scratch_shapes=[pltpu.VMEM((seq * C_GROUP, 2 * HEAD_DIM), BF16), key_copy, val_copy, key_copy, val_copy,
                        pltpu.VMEM((seq + CMP_LEN, HEAD_DIM), F32),
                        pltpu.VMEM((seq + CMP_LEN, HEAD_DIM), F32),
                        pltpu.VMEM((ATT_TILE, SLC_CHUNK), F32),
                        pltpu.VMEM((ATT_TILE, WIN + ATT_TILE), F32)],
        compiler_params=_cparams("parallel", "parallel"),
        name="nsa",
    )(_slope_pieces(slopes), proj, q_gain.reshape(1, -1), k_gain, cmp_pe, cmp_w1.astype(BF16), cmp_w2.astype(BF16),
      jnp.asarray(_nsa_overlap_t(n_blk, n_cmp)))


def _nsa_in_weight(w_in, d_model):
    kv_w = C_KV_HEADS * HEAD_DIM
    cols = []
    for g in range(C_KV_HEADS):
        idx = list(range(g * C_GROUP * HEAD_DIM, (g + 1) * C_GROUP * HEAD_DIM))
        for part in range(6):
            base = d_model + part * kv_w + g * HEAD_DIM
            idx += list(range(base, base + HEAD_DIM))
        gate0 = d_model + 6 * kv_w
        idx += [gate0 + n * C_KV_HEADS * C_GROUP + g * C_GROUP + j for n in range(3) for j in range(C_GROUP)]
        cols.append(jnp.pad(w_in[:, np.asarray(idx)], ((0, 0), (0, NSA_GROUP_COLS - len(idx)))))
    return jnp.concatenate(cols, axis=1)


def kernel(x, attn_norm, ffn_norm, hy_w_in, hy_lb_logits, hy_o_gain, hy_q_gain, hy_k_gain, hy_w_out, nsa_w_in, nsa_q_gain, nsa_k_gain, nsa_cmp_pe, nsa_cmp_w1, nsa_cmp_w2, nsa_w_out, ffn_w_gate, ffn_w_up, ffn_w_down):
    batch, seq, d_model = x.shape
    depth = attn_norm.shape[0]
    a_width = hy_o_gain.shape[1]
    dff = ffn_w_gate.shape[2]
    h = x.reshape(batch * seq, d_model)
    for layer in range(depth):
        i = layer // 2
        if layer % 2 == 0:
            proj = _norm_matmul(h, attn_norm[layer], hy_w_in[i].astype(BF16), tm=1024, tn=512)
            a_out = _hgrn2(proj, hy_lb_logits, hy_o_gain[i], layer_idx=i, batch=batch, seq=seq,
                           heads_per_step=4, chunks_per_step=2)
            b_out = _dilated(proj, hy_q_gain[i], hy_k_gain[i], batch=batch, seq=seq,
                             col0=4 * a_width // (2 * HEAD_DIM))
            w_out = hy_w_out[i].astype(BF16)
            h = _proj_residual(h, [a_out, b_out], [w_out[:a_width], w_out[a_width:]], tm=1024)
        else:
            proj = _norm_matmul(h, attn_norm[layer], _nsa_in_weight(nsa_w_in[i], d_model).astype(BF16),
                                tm=1024, tn=NSA_GROUP_COLS)
            o = _nsa(proj, nsa_q_gain[i], nsa_k_gain[i], nsa_cmp_pe[i], nsa_cmp_w1[i], nsa_cmp_w2[i],
                     batch=batch, seq=seq)
            h = _proj_residual(h, [o], [nsa_w_out[i].astype(BF16)], tm=1024)
        h = _ffn(h, ffn_norm[layer], ffn_w_gate[layer].astype(BF16), ffn_w_up[layer].astype(BF16),
                 ffn_w_down[layer].astype(BF16), tm=512, tf=dff // 2)
    return h.reshape(batch, seq, d_model)
```

```python
import functools

import numpy as np
import jax
import jax.numpy as jnp
from jax import lax
from jax.experimental import pallas as pl
from jax.experimental.pallas import tpu as pltpu

F32 = jnp.float32
BF16 = jnp.bfloat16

EPS = 1e-6
HEAD_DIM = 64
ATTN_SCALE = HEAD_DIM ** -0.5
NEG = -1e30

A_EXPAND = 128
A_CHUNK = 32
A_SUB = 8

DILATED_PATTERNS = ((128, 1), (512, 4), (2048, 16))
ATT_TILE = 128

C_KV_HEADS = 4
C_GROUP = 4
CMP_LEN = 32
CMP_STRIDE = 16
CMP_HIDDEN = 2 * HEAD_DIM
SLC_BLOCK = 64
SLC_TOPN = 8
SLC_CHUNK = 512
WIN = 512
NSA_GROUP_COLS = 768
NSA_GATE_COL = 640

V7X_VMEM_LIMIT = 56 * 1024 * 1024


def _cparams(*sem):
    return pltpu.CompilerParams(dimension_semantics=sem, vmem_limit_bytes=V7X_VMEM_LIMIT)


def _rms(x, gain):
    return x * lax.rsqrt(jnp.mean(x * x, axis=-1, keepdims=True) + EPS) * gain


def _silu(x):
    return x * jax.nn.sigmoid(x)


def _split3(x):
    hi = x.astype(BF16)
    r1 = x - hi.astype(F32)
    mid = r1.astype(BF16)
    lo = (r1 - mid.astype(F32)).astype(BF16)
    return hi, mid, lo


def _dot(a, b):
    return jnp.dot(a, b, preferred_element_type=F32)


def _dot_nt(a, b):
    return lax.dot_general(a, b, (((1,), (1,)), ((), ())), preferred_element_type=F32)


def _dot_tn(a, b):
    return lax.dot_general(a, b, (((0,), (0,)), ((), ())), preferred_element_type=F32)


def _norm_matmul_kernel(x_ref, g_ref, w_ref, o_ref, h_ref):
    @pl.when(pl.program_id(1) == 0)
    def _():
        h_ref[...] = _rms(x_ref[...], g_ref[...]).astype(BF16)

    o_ref[...] = _dot(h_ref[...], w_ref[...]).astype(o_ref.dtype)


def _norm_matmul(x, gain, w, *, tm, tn):
    t, d = x.shape
    n = w.shape[1]
    return pl.pallas_call(
        _norm_matmul_kernel,
        grid=(t // tm, n // tn),
        in_specs=[pl.BlockSpec((tm, d), lambda i, j: (i, 0)),
                  pl.BlockSpec((1, d), lambda i, j: (0, 0)),
                  pl.BlockSpec((d, tn), lambda i, j: (0, j))],
        out_specs=pl.BlockSpec((tm, tn), lambda i, j: (i, j)),
        out_shape=jax.ShapeDtypeStruct((t, n), BF16),
        scratch_shapes=[pltpu.VMEM((tm, d), BF16)],
        compiler_params=_cparams("parallel", "arbitrary"),
        name="norm_matmul",
    )(x, gain.reshape(1, d), w)


def _mix_ffn_kernel(*refs, n_in):
    x_ref = refs[0]
    part_refs, w_refs = refs[1:1 + n_in], refs[1 + n_in:1 + 2 * n_in]
    g_ref, wg_ref, wu_ref, wd_ref, o_ref, h_ref, acc_ref = refs[1 + 2 * n_in:]
    f = pl.program_id(1)

    @pl.when(f == 0)
    def _():
        x = x_ref[...]
        for a_ref, w_ref in zip(part_refs, w_refs):
            x = x + _dot(a_ref[...], w_ref[...])
        h_ref[...] = _rms(x, g_ref[...]).astype(BF16)
        acc_ref[...] = x

    h = h_ref[...]
    act = _silu(_dot(h, wg_ref[...])) * _dot(h, wu_ref[...])
    acc_ref[...] += _dot(act.astype(BF16), wd_ref[...])

    @pl.when(f == pl.num_programs(1) - 1)
    def _():
        o_ref[...] = acc_ref[...]


def _mix_ffn(x, parts, weights, gain, w_gate, w_up, w_down, *, tm, tf):
    t, d = x.shape
    dff = w_gate.shape[1]
    n_in = len(parts)
    in_specs = [pl.BlockSpec((tm, d), lambda i, f: (i, 0))]
    in_specs += [pl.BlockSpec((tm, p.shape[1]), lambda i, f: (i, 0)) for p in parts]
    in_specs += [pl.BlockSpec(w.shape, lambda i, f: (0, 0)) for w in weights]
    in_specs += [pl.BlockSpec((1, d), lambda i, f: (0, 0)),
                 pl.BlockSpec((d, tf), lambda i, f: (0, f)),
                 pl.BlockSpec((d, tf), lambda i, f: (0, f)),
                 pl.BlockSpec((tf, d), lambda i, f: (f, 0))]
    return pl.pallas_call(
        functools.partial(_mix_ffn_kernel, n_in=n_in),
        grid=(t // tm, dff // tf),
        in_specs=in_specs,
        out_specs=pl.BlockSpec((tm, d), lambda i, f: (i, 0)),
        out_shape=jax.ShapeDtypeStruct((t, d), F32),
        scratch_shapes=[pltpu.VMEM((tm, d), BF16), pltpu.VMEM((tm, d), F32)],
        compiler_params=_cparams("parallel", "arbitrary"),
        name="mix_ffn",
    )(x, *parts, *weights, gain.reshape(1, d), w_gate, w_up, w_down)


def _hgrn2_kernel(q_ref, f_ref, v_ref, g_ref, lbl_ref, og_ref, o_ref, state_ref, *,
                  layer_idx, seq, heads, unroll):
    c_len, hd = A_CHUNK, A_EXPAND
    logits = lbl_ref[...]
    e = jnp.exp(logits - jnp.max(logits, axis=0, keepdims=True))
    prob = e / jnp.sum(e, axis=0, keepdims=True)
    lb_all = jnp.sum(prob[:layer_idx + 1], axis=0, keepdims=True) - prob[0:1]
    og_all = og_ref[...]

    sub, n_sub = A_SUB, c_len // A_SUB
    row = lax.broadcasted_iota(jnp.int32, (c_len, hd), 0)
    row_sub = lax.broadcasted_iota(jnp.int32, (sub, hd), 0)
    lane_sub = lax.broadcasted_iota(jnp.int32, (sub, c_len), 1)
    tril = (lax.broadcasted_iota(jnp.int32, (c_len, c_len), 0)
            >= lax.broadcasted_iota(jnp.int32, (c_len, c_len), 1)).astype(BF16)
    ones = jnp.ones((hd, hd), BF16)

    state_ref[...] = jnp.zeros_like(state_ref)

    def step(c, carry):
        chains = [(h, u) for h in range(heads) for u in range(unroll)]
        rows = {u: pl.ds(pl.multiple_of((c * unroll + u) * c_len, c_len), c_len) for u in range(unroll)}
        cols = {h: slice(h * hd, (h + 1) * hd) for h in range(heads)}

        qq, kk, v, b = {}, {}, {}, {}
        for ch in chains:
            h, u = ch
            lb = lb_all[:, cols[h]]
            f = lb + (1.0 - lb) * jax.nn.sigmoid(f_ref[rows[u], cols[h]].astype(F32))
            kk[ch] = 1.0 - f
            qq[ch] = _silu(q_ref[rows[u], cols[h]].astype(F32))
            v[ch] = v_ref[rows[u], cols[h]].astype(BF16)
            hi, mid, lo = _split3(jnp.log(f))
            b[ch] = _dot(tril, hi) + _dot(tril, mid) + _dot(tril, lo)

        r = {}
        for ch in chains:
            pieces = []
            for i in range(n_sub):
                lo_ = i * sub
                for s in range(sub):
                    d = b[ch][lo_:lo_ + sub] - b[ch][lo_ + s:lo_ + s + 1]
                    e_s = jnp.exp(jnp.where(row_sub >= s, d, NEG))
                    pieces.append(qq[ch][lo_:lo_ + sub] * e_s * kk[ch][lo_ + s:lo_ + s + 1])
            r[ch] = _dot(jnp.concatenate(pieces, axis=0).astype(BF16), ones)[:, :c_len]

        o = {}
        for ch in chains:
            blocks = []
            for i in range(n_sub):
                lo_ = i * sub
                sc = jnp.zeros((sub, c_len), F32)
                for s in range(sub):
                    at = (lo_ + s) * sub
                    sc = jnp.where(lane_sub == lo_ + s, r[ch][at:at + sub], sc)
                if i > 0:
                    b_ref_row = b[ch][lo_ - 1:lo_]
                    q_dec = (qq[ch][lo_:lo_ + sub] * jnp.exp(b[ch][lo_:lo_ + sub] - b_ref_row)).astype(BF16)
                    k_grow = (kk[ch] * jnp.exp(jnp.where(row < lo_, b_ref_row - b[ch], NEG))).astype(BF16)
                    sc = sc + _dot_nt(q_dec, k_grow)
                blocks.append(sc)
            o[ch] = _dot(jnp.concatenate(blocks, axis=0).astype(BF16), v[ch])

        upd, q_all = {}, {}
        for ch in chains:
            b_last = b[ch][c_len - 1:c_len]
            k_dec = (kk[ch] * jnp.exp(b_last - b[ch])).astype(BF16)
            upd[ch] = (jnp.exp(b_last), _dot_tn(v[ch], k_dec))
            q_all[ch] = (qq[ch] * jnp.exp(b[ch])).astype(BF16)

        for h in range(heads):
            st = state_ref[h]
            for u in range(unroll):
                ch = (h, u)
                o_c = o[ch] + _dot_nt(q_all[ch], st.astype(BF16))
                st = st * upd[ch][0] + upd[ch][1]
                o_c = o_c * lax.rsqrt(jnp.mean(o_c * o_c, axis=-1, keepdims=True) + EPS)
                gate = _silu(g_ref[rows[u], cols[h]].astype(F32))
                o_ref[rows[u], cols[h]] = (o_c * og_all[:, cols[h]] * gate).astype(BF16)
            state_ref[h] = st
        return carry

    lax.fori_loop(0, seq // (c_len * unroll), step, 0)


def _hgrn2(proj, lb_logits, o_gain, *, layer_idx, batch, seq, heads_per_step, chunks_per_step):
    heads = o_gain.shape[0] // A_EXPAND
    groups = heads // heads_per_step
    width = heads_per_step * A_EXPAND
    n_layers = lb_logits.shape[0]

    def col(off):
        return pl.BlockSpec((seq, width), lambda b, h: (b, h + off * groups))

    return pl.pallas_call(
        functools.partial(_hgrn2_kernel, layer_idx=layer_idx, seq=seq, heads=heads_per_step,
                          unroll=chunks_per_step),
        grid=(batch, groups),
        in_specs=[col(0), col(1), col(2), col(3),
                  pl.BlockSpec((n_layers, width), lambda b, h: (0, h)),
                  pl.BlockSpec((1, width), lambda b, h: (0, h))],
        out_specs=pl.BlockSpec((seq, width), lambda b, h: (b, h)),
        out_shape=jax.ShapeDtypeStruct((batch * seq, heads * A_EXPAND), BF16),
        scratch_shapes=[pltpu.VMEM((heads_per_step, A_EXPAND, A_EXPAND), F32)],
        compiler_params=_cparams("parallel", "parallel"),
        name="hgrn2",
    )(proj, proj, proj, proj, lb_logits, o_gain.reshape(1, -1))


DIL_NEAR = DILATED_PATTERNS[:2]
DIL_FAR = DILATED_PATTERNS[2]
DIL_BACK = max(w for w, _ in DIL_NEAR)


def _dilated_near_log_weights():
    r = np.arange(ATT_TILE)[:, None]
    c = np.arange(DIL_BACK + ATT_TILE)[None, :]
    d = r + DIL_BACK - c
    mult = np.zeros_like(d)
    for window, dilation in DIL_NEAR:
        mult += (d >= 0) & (d % dilation == 0) & (d <= window)
    return np.where(mult > 0, np.log2(np.maximum(mult, 1)), NEG).astype(np.float32)


FAR_MAX_LANE = 96
LOG2E = 1.4426950408889634
FAR_UNROLL = 2
NEAR_UNROLL = 2
POS_SHIFT = 7
POS_RADIX = 1 << POS_SHIFT


def _tile_rows(first_tile, n_tiles):
    start = first_tile * ATT_TILE
    if not isinstance(first_tile, int):
        start = pl.multiple_of(start, ATT_TILE)
    return pl.ds(start, n_tiles * ATT_TILE)


def _softmax_blocks(qs, ks, biases, vs):
    return _softmax_of([_dot_nt(q, k) + bias for q, k, bias in zip(qs, ks, biases)], vs)


def _softmax_of(a, vs):
    m = [jnp.max(x, axis=-1, keepdims=True) for x in a]
    p = [jnp.exp2(x - mm) for x, mm in zip(a, m)]
    acc = [_dot(x.astype(BF16), v) for x, v in zip(p, vs)]
    return list(zip(m, acc))


def _online_softmax_of(a, v_tile, states):
    m_new = [jnp.maximum(st[0], jnp.max(x, axis=-1, keepdims=True)) for x, st in zip(a, states)]
    alpha = [jnp.exp2(st[0] - mn) for st, mn in zip(states, m_new)]
    p = [jnp.exp2(x - mn) for x, mn in zip(a, m_new)]
    pv = [_dot(x.astype(BF16), v_tile) for x in p]
    return tuple((mn, al * st[1] + o) for mn, al, st, o in zip(m_new, alpha, states, pv))


def _normalised(acc):
    return acc[:, :HEAD_DIM] / acc[:, HEAD_DIM:HEAD_DIM + 1]


def _with_ones(v):
    lane = lax.broadcasted_iota(jnp.int32, v.shape, 1)
    return jnp.concatenate([v, jnp.where(lane == 0, 1.0, 0.0)], axis=1)


def _slope_pieces(slopes):
    hi = slopes.astype(BF16).astype(F32)
    mid = (slopes - hi).astype(BF16).astype(F32)
    lo = (slopes - hi - mid).astype(BF16).astype(F32)
    return jnp.stack([hi, mid, lo], axis=1).reshape(-1)


def _alibi_query_cols(pieces, rows):
    lane = lax.broadcasted_iota(jnp.int32, (rows, HEAD_DIM), 1)
    out = jnp.zeros((rows, HEAD_DIM), F32)
    for n, piece in enumerate(pieces):
        out = jnp.where(lane == n, piece * POS_RADIX, jnp.where(lane == n + 3, piece, out))
    return out


def _alibi_key_cols(pos):
    lane = lax.broadcasted_iota(jnp.int32, pos.shape, 1)
    hi = lax.shift_right_logical(pos, POS_SHIFT).astype(F32)
    lo = jnp.bitwise_and(pos, POS_RADIX - 1).astype(F32)
    return jnp.where(lane < 3, hi, jnp.where(lane < 6, lo, 0.0))


def _dilated_kernel(pieces_ref, q_ref, k_ref, v_ref, qg_ref, kg_ref, lw_ref, o_ref,
                    qn_ref, kn_ref, vb_ref, qr_ref, kr_ref, vr_ref, n32_ref, far_ref, *, seq):
    tile = ATT_TILE
    pair = pl.program_id(1)
    n_tiles = seq // tile
    back = DIL_BACK // tile
    stride = DIL_FAR[1]
    n_res = seq // stride
    far_mask = jnp.where(lax.broadcasted_iota(jnp.int32, (n_res, n_res), 0)
                         >= lax.broadcasted_iota(jnp.int32, (n_res, n_res), 1), 0.0, NEG)
    pos = lax.broadcasted_iota(jnp.int32, (seq, HEAD_DIM), 0)

    def copies(hh, values, nat_ref, res_ref):
        n32_ref[...] = values
        nat_ref[hh] = values.astype(BF16)
        for r in range(stride):
            res_ref[hh, r * n_res:(r + 1) * n_res, :] = n32_ref[pl.ds(r, n_res, stride=stride), :].astype(BF16)

    for hh in range(2):
        lanes = slice(hh * HEAD_DIM, (hh + 1) * HEAD_DIM)
        pieces = [pieces_ref[(2 * pair + hh) * 3 + n] for n in range(3)]
        q_aug = jnp.concatenate([_rms(q_ref[:, lanes].astype(F32), qg_ref[...]) * (ATTN_SCALE * LOG2E),
                                 _alibi_query_cols(pieces, seq)], axis=1)
        k_aug = jnp.concatenate([_rms(k_ref[:, lanes].astype(F32), kg_ref[...]), _alibi_key_cols(pos)], axis=1)
        copies(hh, q_aug, qn_ref, qr_ref)
        copies(hh, k_aug, kn_ref, kr_ref)
        copies(hh, _with_ones(v_ref[:, lanes].astype(F32)), vb_ref, vr_ref)

    def far_step(step, carry):
        res = [FAR_UNROLL * step + n for n in range(FAR_UNROLL)]
        units = [(hh, n) for n in range(FAR_UNROLL) for hh in range(2)]
        rows = [pl.ds(pl.multiple_of(r * n_res, n_res), n_res) for r in res]
        parts = _softmax_blocks([qr_ref[hh, rows[n], :] for hh, n in units],
                                [kr_ref[hh, rows[n], :] for hh, n in units],
                                [far_mask] * len(units),
                                [vr_ref[hh, rows[n], :] for hh, n in units])
        for (hh, n), (m, acc) in zip(units, parts):
            far_ref[hh, pl.ds(res[n], n_res, stride=stride), :] = jnp.concatenate(
                [acc[:, :FAR_MAX_LANE], jnp.broadcast_to(m, (n_res, 2 * HEAD_DIM - FAR_MAX_LANE))], axis=1)
        return carry

    lax.fori_loop(0, stride // FAR_UNROLL, far_step, 0)

    def near_tiles(tiles):
        units = [(hh, t) for t in range(len(tiles)) for hh in range(2)]
        rows = [_tile_rows(i, 1) for i, _, _ in tiles]
        krows = [_tile_rows(first, n_win) for _, first, n_win in tiles]
        bias = [lw_ref[:, (back + 1 - n_win) * tile:] for _, _, n_win in tiles]
        parts = _softmax_blocks([qn_ref[hh, rows[t], :] for hh, t in units],
                                [kn_ref[hh, krows[t], :] for hh, t in units],
                                [bias[t] for hh, t in units],
                                [vb_ref[hh, krows[t], :] for hh, t in units])
        outs = {}
        for (hh, t), (m1, acc1) in zip(units, parts):
            far = far_ref[hh, rows[t], :]
            m2 = far[:, FAR_MAX_LANE:FAR_MAX_LANE + 1]
            m = jnp.maximum(m1, m2)
            outs[hh, t] = _normalised(jnp.exp2(m1 - m) * acc1 + jnp.exp2(m2 - m) * far)
        for t in range(len(tiles)):
            o_ref[rows[t], :] = jnp.concatenate([outs[0, t], outs[1, t]], axis=1).astype(BF16)

    head_tiles = min(back, n_tiles)
    near_tiles([(i, 0, i + 1) for i in range(head_tiles)])

    def near_body(step, carry):
        first = head_tiles + NEAR_UNROLL * step
        near_tiles([(first + n, first + n - back, back + 1) for n in range(NEAR_UNROLL)])
        return carry

    lax.fori_loop(0, (n_tiles - head_tiles) // NEAR_UNROLL, near_body, 0)


def _dilated(proj, q_gain, k_gain, *, batch, seq, col0):
    pairs = 4
    heads = 2 * pairs
    slopes = (2.0 ** (-8.0 * jnp.arange(1, heads + 1, dtype=F32) / heads)).astype(F32)
    assert seq <= DIL_FAR[0] and seq % (DIL_FAR[1] * 8) == 0 and seq % ATT_TILE == 0
    assert seq <= POS_RADIX * 256 and DIL_FAR[1] % FAR_UNROLL == 0
    assert (seq // ATT_TILE - min(DIL_BACK // ATT_TILE, seq // ATT_TILE)) % NEAR_UNROLL == 0
    win = DIL_BACK + ATT_TILE

    def col(off):
        return pl.BlockSpec((seq, 2 * HEAD_DIM), lambda b, p: (b, col0 + off * pairs + p))

    head_copy = pltpu.VMEM((2, seq, 2 * HEAD_DIM), BF16)
    return pl.pallas_call(
        functools.partial(_dilated_kernel, seq=seq),
        grid=(batch, pairs),
        in_specs=[pl.BlockSpec(memory_space=pltpu.SMEM),
                  col(0), col(1), col(2),
                  pl.BlockSpec((1, HEAD_DIM), lambda b, p: (0, 0)),
                  pl.BlockSpec((1, HEAD_DIM), lambda b, p: (0, 0)),
                  pl.BlockSpec((ATT_TILE, win), lambda b, p: (0, 0))],
        out_specs=pl.BlockSpec((seq, 2 * HEAD_DIM), lambda b, p: (b, p)),
        out_shape=jax.ShapeDtypeStruct((batch * seq, heads * HEAD_DIM), BF16),
        scratch_shapes=[head_copy] * 6 + [pltpu.VMEM((seq, 2 * HEAD_DIM), F32),
                                          pltpu.VMEM((2, seq, 2 * HEAD_DIM), F32)],
        compiler_params=_cparams("parallel", "parallel"),
        name="dilated",
    )(_slope_pieces(slopes * LOG2E), proj, proj, proj, q_gain.reshape(1, -1), k_gain.reshape(1, -1),
      jnp.asarray(_dilated_near_log_weights()))


def _nsa_overlap_t(n_blocks, n_cmp_rows):
    blk = np.arange(n_blocks)[:, None]
    start = np.arange(n_cmp_rows)[None, :] * CMP_STRIDE
    return ((start < (blk + 1) * SLC_BLOCK) & (start + CMP_LEN > blk * SLC_BLOCK)).astype(np.float32)


def _nsa_kernel(pieces_ref, x_ref, qg_ref, kg_ref, pe_ref, w1_ref, w2_ref, ovl_ref, o_ref,
                qn_ref, ks_ref, vs_ref, kw_ref, vw_ref, kc_ref, vc_ref, rc_slc_ref, rc_win_ref, *, seq):
    tile = ATT_TILE
    n_tiles = seq // tile
    n_cmp = seq // CMP_STRIDE
    n_blk = seq // SLC_BLOCK
    grp = pl.program_id(1)
    heads = range(C_GROUP)

    q_cols = [_alibi_query_cols([pieces_ref[(grp * C_GROUP + j) * 3 + n] for n in range(3)], tile)
              for j in heads]
    row_pos = lax.broadcasted_iota(jnp.int32, (tile, HEAD_DIM), 0)

    def prep(i, carry):
        rows = pl.ds(pl.multiple_of(i * tile, tile), tile)
        k_cols = _alibi_key_cols(i * tile + row_pos)
        x = x_ref[rows, :NSA_GATE_COL].astype(F32)
        for j in heads:
            qj = _rms(x[:, j * HEAD_DIM:(j + 1) * HEAD_DIM], qg_ref[...]) * (ATTN_SCALE * LOG2E)
            qn_ref[pl.ds(pl.multiple_of((i * C_GROUP + j) * tile, tile), tile), :] = (
                jnp.concatenate([qj, q_cols[j]], axis=1).astype(BF16))
        kc_ref[rows, :] = x[:, 256:320]
        vc_ref[rows, :] = x[:, 320:384]
        ks_ref[rows, :] = jnp.concatenate([_rms(x[:, 384:448], kg_ref[1:2]), k_cols], axis=1).astype(BF16)
        vs_ref[rows, :] = _with_ones(x[:, 448:512]).astype(BF16)
        kw_ref[rows, :] = jnp.concatenate([_rms(x[:, 512:576], kg_ref[2:3]), k_cols], axis=1).astype(BF16)
        vw_ref[rows, :] = _with_ones(x[:, 576:640]).astype(BF16)
        return carry

    lax.fori_loop(0, n_tiles, prep, 0)
    kc_ref[seq:, :] = jnp.zeros((CMP_LEN, HEAD_DIM), F32)
    vc_ref[seq:, :] = jnp.zeros((CMP_LEN, HEAD_DIM), F32)

    def compress(src_ref, which):
        hid = jnp.zeros((n_cmp, CMP_HIDDEN), F32)
        for pos in range(CMP_LEN):
            tok = src_ref[pl.ds(pos, n_cmp, stride=CMP_STRIDE), :] + pe_ref[which, pos:pos + 1, :]
            hid = hid + _dot(tok.astype(BF16), w1_ref[which, pos * HEAD_DIM:(pos + 1) * HEAD_DIM, :])
        return _dot(_silu(hid).astype(BF16), w2_ref[which])

    cmp_pos = lax.broadcasted_iota(jnp.int32, (n_cmp, HEAD_DIM), 0) * CMP_STRIDE + (CMP_LEN - 1)
    k_cmp = jnp.concatenate([_rms(compress(kc_ref, 0), kg_ref[0:1]), _alibi_key_cols(cmp_pos)],
                            axis=1).astype(BF16)
    v_cmp = compress(vc_ref, 1).astype(BF16)
    ovl = ovl_ref[...].astype(BF16)

    rc_slc_ref[...] = (lax.broadcasted_iota(jnp.int32, (tile, SLC_CHUNK), 0)
                       - lax.broadcasted_iota(jnp.int32, (tile, SLC_CHUNK), 1)).astype(F32)
    rc_win_ref[...] = (lax.broadcasted_iota(jnp.int32, (tile, WIN + tile), 0)
                       - lax.broadcasted_iota(jnp.int32, (tile, WIN + tile), 1)).astype(F32)
    blk_row = lax.broadcasted_iota(jnp.int32, (tile, SLC_CHUNK), 0)
    blk_of_key = lax.broadcasted_iota(jnp.int32, (tile, SLC_CHUNK), 1) // SLC_BLOCK
    cmp_end = (lax.broadcasted_iota(jnp.int32, (tile, n_cmp), 1) * CMP_STRIDE + (CMP_LEN - 1))
    blk_i = lax.broadcasted_iota(jnp.int32, (n_blk, tile), 0)
    tok_lane = lax.broadcasted_iota(jnp.int32, (n_blk, tile), 1)

    def head_rows(s):
        return [s[j * tile:(j + 1) * tile] for j in heads]

    def q_tile(i, carry):
        t0 = i * tile
        rows = pl.ds(pl.multiple_of(t0, tile), tile)
        q4 = qn_ref[pl.ds(pl.multiple_of(i * C_GROUP * tile, tile), C_GROUP * tile), :]

        vis_c = (t0 + lax.broadcasted_iota(jnp.int32, (tile, n_cmp), 0)) >= cmp_end
        a_c = [jnp.where(vis_c, x, NEG) for x in head_rows(_dot_nt(q4, k_cmp))]
        m_c = [jnp.max(x, axis=-1, keepdims=True) for x in a_c]
        e_c = [jnp.where(vis_c, jnp.exp2(x - m), 0.0) for x, m in zip(a_c, m_c)]
        l_c = [jnp.sum(x, axis=-1, keepdims=True) for x in e_c]
        p_c = [x / jnp.where(l > 0, l, 1.0) for x, l in zip(e_c, l_c)]
        o_cmp = [_dot(x.astype(BF16), v_cmp) for x in p_c]
        hi, mid, lo = _split3(p_c[0] + p_c[1] + p_c[2] + p_c[3])
        imp = _dot_nt(ovl, hi) + _dot_nt(ovl, mid) + _dot_nt(ovl, lo)

        cur = lax.shift_right_logical(t0 + tok_lane, 6)
        forced = (blk_i == 0) | (blk_i == cur) | (blk_i == cur - 1)
        score = jnp.where(blk_i > cur, -1.0, jnp.where(forced, 1e30, imp))
        rank = jnp.zeros((n_blk, tile), F32)
        for other in range(n_blk):
            s_o = score[other:other + 1, :]
            rank = rank + jnp.where(s_o > score, 1.0,
                                    jnp.where(s_o == score, jnp.where(blk_i > other, 1.0, 0.0), 0.0))
        sel = jnp.where(rank < SLC_TOPN, jnp.where(score >= 0, 1.0, 0.0), 0.0)
        sel_t = jnp.concatenate([sel, jnp.zeros((tile - n_blk, tile), F32)], axis=0).T.astype(BF16)

        def slc_step(c, st):
            krows = pl.ds(pl.multiple_of(c * SLC_CHUNK, SLC_CHUNK), SLC_CHUNK)
            expand = jnp.where(blk_row == c * (SLC_CHUNK // SLC_BLOCK) + blk_of_key, 1.0, 0.0).astype(BF16)
            picked = _dot(sel_t, expand)
            dist = rc_slc_ref[...] + jnp.asarray(t0 - c * SLC_CHUNK, F32)
            mask = jnp.where(picked > 0.5, jnp.where(dist >= 0, 0.0, NEG), NEG)
            a = [x + mask for x in head_rows(_dot_nt(q4, ks_ref[krows, :]))]
            return _online_softmax_of(a, vs_ref[krows, :], st)

        init = tuple((jnp.full((tile, 1), NEG, F32), jnp.zeros((tile, 2 * HEAD_DIM), F32)) for _ in heads)
        slc = lax.fori_loop(0, t0 // SLC_CHUNK + 1, slc_step, init)

        first = jnp.maximum(i - WIN // tile, 0)
        krows = pl.ds(pl.multiple_of(first * tile, tile), WIN + tile)
        dist = rc_win_ref[...] + jnp.asarray((i - first) * tile, F32)
        mask = jnp.where(dist >= 0, jnp.where(dist < WIN, 0.0, NEG), NEG)
        win = _softmax_of([x + mask for x in head_rows(_dot_nt(q4, kw_ref[krows, :]))],
                          [vw_ref[krows, :]] * C_GROUP)

        gate = jax.nn.sigmoid(x_ref[rows, NSA_GATE_COL:NSA_GATE_COL + 128].astype(F32))
        outs = []
        for j in heads:
            outs.append(gate[:, j:j + 1] * o_cmp[j]
                        + gate[:, C_GROUP + j:C_GROUP + j + 1] * _normalised(slc[j][1])
                        + gate[:, 2 * C_GROUP + j:2 * C_GROUP + j + 1] * _normalised(win[j][1]))
        o_ref[rows, :] = jnp.concatenate(outs, axis=1).astype(BF16)
        return carry

    lax.fori_loop(0, n_tiles, q_tile, 0)


def _nsa(proj, q_gain, k_gain, cmp_pe, cmp_w1, cmp_w2, *, batch, seq):
    heads = C_KV_HEADS * C_GROUP
    slopes = (2.0 ** (-8.0 * jnp.arange(1, heads + 1, dtype=F32) / heads)).astype(F32)
    n_cmp = seq // CMP_STRIDE
    n_blk = seq // SLC_BLOCK
    assert seq % SLC_CHUNK == 0 and seq >= WIN + ATT_TILE and n_blk <= ATT_TILE and n_cmp <= ATT_TILE
    full = lambda *shape: pl.BlockSpec(shape, lambda b, g: (0,) * len(shape))
    head_copy = pltpu.VMEM((seq, 2 * HEAD_DIM), BF16)
    return pl.pallas_call(
        functools.partial(_nsa_kernel, seq=seq),
        grid=(batch, C_KV_HEADS),
        in_specs=[pl.BlockSpec(memory_space=pltpu.SMEM),
                  pl.BlockSpec((seq, NSA_GROUP_COLS), lambda b, g: (b, g)),
                  full(1, HEAD_DIM), full(3, HEAD_DIM), full(2, CMP_LEN, HEAD_DIM),
                  full(2, CMP_LEN * HEAD_DIM, CMP_HIDDEN), full(2, CMP_HIDDEN, HEAD_DIM),
                  full(n_blk, n_cmp)],
        out_specs=pl.BlockSpec((seq, C_GROUP * HEAD_DIM), lambda b, g: (b, g)),
        out_shape=jax.ShapeDtypeStruct((batch * seq, heads * HEAD_DIM), BF16),
        scratch_shapes=[pltpu.VMEM((seq * C_GROUP, 2 * HEAD_DIM), BF16)] + [head_copy] * 4 + [
            pltpu.VMEM((seq + CMP_LEN, HEAD_DIM), F32),
            pltpu.VMEM((seq + CMP_LEN, HEAD_DIM), F32),
            pltpu.VMEM((ATT_TILE, SLC_CHUNK), F32),
            pltpu.VMEM((ATT_TILE, WIN + ATT_TILE), F32)],
        compiler_params=_cparams("parallel", "parallel"),
        name="nsa",
    )(_slope_pieces(slopes * LOG2E), proj, q_gain.reshape(1, -1), k_gain, cmp_pe,
      cmp_w1.astype(BF16), cmp_w2.astype(BF16), jnp.asarray(_nsa_overlap_t(n_blk, n_cmp)))


def _nsa_in_weight(w_in, d_model):
    kv_w = C_KV_HEADS * HEAD_DIM
    cols = []
    for g in range(C_KV_HEADS):
        idx = list(range(g * C_GROUP * HEAD_DIM, (g + 1) * C_GROUP * HEAD_DIM))
        for part in range(6):
            base = d_model + part * kv_w + g * HEAD_DIM
            idx += list(range(base, base + HEAD_DIM))
        gate0 = d_model + 6 * kv_w
        idx += [gate0 + n * C_KV_HEADS * C_GROUP + g * C_GROUP + j for n in range(3) for j in range(C_GROUP)]
        cols.append(jnp.pad(w_in[:, np.asarray(idx)], ((0, 0), (0, NSA_GROUP_COLS - len(idx)))))
    return jnp.concatenate(cols, axis=1)


def kernel(x, attn_norm, ffn_norm, hy_w_in, hy_lb_logits, hy_o_gain, hy_q_gain, hy_k_gain, hy_w_out, nsa_w_in, nsa_q_gain, nsa_k_gain, nsa_cmp_pe, nsa_cmp_w1, nsa_cmp_w2, nsa_w_out, ffn_w_gate, ffn_w_up, ffn_w_down):
    batch, seq, d_model = x.shape
    depth = attn_norm.shape[0]
    a_width = hy_o_gain.shape[1]
    dff = ffn_w_gate.shape[2]
    h = x.reshape(batch * seq, d_model)
    for layer in range(depth):
        i = layer // 2
        if layer % 2 == 0:
            proj = _norm_matmul(h, attn_norm[layer], hy_w_in[i].astype(BF16), tm=1024, tn=512)
            a_out = _hgrn2(proj, hy_lb_logits, hy_o_gain[i], layer_idx=i, batch=batch, seq=seq,
                           heads_per_step=4, chunks_per_step=2)
            b_out = _dilated(proj, hy_q_gain[i], hy_k_gain[i], batch=batch, seq=seq,
                             col0=4 * a_width // (2 * HEAD_DIM))
            w_out = hy_w_out[i].astype(BF16)
            mixed, w_mix = [a_out, b_out], [w_out[:a_width], w_out[a_width:]]
        else:
            proj = _norm_matmul(h, attn_norm[layer], _nsa_in_weight(nsa_w_in[i], d_model).astype(BF16),
                                tm=1024, tn=NSA_GROUP_COLS)
            o = _nsa(proj, nsa_q_gain[i], nsa_k_gain[i], nsa_cmp_pe[i], nsa_cmp_w1[i], nsa_cmp_w2[i],
                     batch=batch, seq=seq)
            mixed, w_mix = [o], [nsa_w_out[i].astype(BF16)]
        h = _mix_ffn(h, mixed, w_mix, ffn_norm[layer], ffn_w_gate[layer].astype(BF16),
                     ffn_w_up[layer].astype(BF16), ffn_w_down[layer].astype(BF16), tm=512, tf=dff // 2)
    return h.reshape(batch, seq, d_model)
```

```python
import functools

import numpy as np
import jax
import jax.numpy as jnp
from jax import lax
from jax.experimental import pallas as pl
from jax.experimental.pallas import tpu as pltpu

F32 = jnp.float32
BF16 = jnp.bfloat16

EPS = 1e-6
HEAD_DIM = 64
ATTN_SCALE = HEAD_DIM ** -0.5
NEG = -1e30

A_EXPAND = 128
A_CHUNK = 32
A_SUB = 8

DILATED_PATTERNS = ((128, 1), (512, 4), (2048, 16))
ATT_TILE = 128

C_KV_HEADS = 4
C_GROUP = 4
CMP_LEN = 32
CMP_STRIDE = 16
CMP_HIDDEN = 2 * HEAD_DIM
SLC_BLOCK = 64
SLC_TOPN = 8
SLC_CHUNK = 512
WIN = 512
NSA_GROUP_COLS = 768
NSA_GATE_COL = 640
NSA_TILE = 256
NSA_PREP_ROWS = 2048

V7X_VMEM_LIMIT = 56 * 1024 * 1024


def _cparams(*sem):
    return pltpu.CompilerParams(dimension_semantics=sem, vmem_limit_bytes=V7X_VMEM_LIMIT)


def _rms(x, gain):
    return x * lax.rsqrt(jnp.mean(x * x, axis=-1, keepdims=True) + EPS) * gain


def _silu(x):
    return x * jax.nn.sigmoid(x)


def _split3(x):
    hi = x.astype(BF16)
    r1 = x - hi.astype(F32)
    mid = r1.astype(BF16)
    lo = (r1 - mid.astype(F32)).astype(BF16)
    return hi, mid, lo


def _dot(a, b):
    return jnp.dot(a, b, preferred_element_type=F32)


def _dot_nt(a, b):
    return lax.dot_general(a, b, (((1,), (1,)), ((), ())), preferred_element_type=F32)


def _dot_tn(a, b):
    return lax.dot_general(a, b, (((0,), (0,)), ((), ())), preferred_element_type=F32)


def _norm_matmul_kernel(x_ref, g_ref, w_ref, o_ref, h_ref):
    @pl.when(pl.program_id(1) == 0)
    def _():
        h_ref[...] = _rms(x_ref[...], g_ref[...]).astype(BF16)

    o_ref[...] = _dot(h_ref[...], w_ref[...]).astype(o_ref.dtype)


def _norm_matmul(x, gain, w, *, tm, tn):
    t, d = x.shape
    n = w.shape[1]
    return pl.pallas_call(
        _norm_matmul_kernel,
        grid=(t // tm, n // tn),
        in_specs=[pl.BlockSpec((tm, d), lambda i, j: (i, 0)),
                  pl.BlockSpec((1, d), lambda i, j: (0, 0)),
                  pl.BlockSpec((d, tn), lambda i, j: (0, j))],
        out_specs=pl.BlockSpec((tm, tn), lambda i, j: (i, j)),
        out_shape=jax.ShapeDtypeStruct((t, n), BF16),
        scratch_shapes=[pltpu.VMEM((tm, d), BF16)],
        compiler_params=_cparams("parallel", "arbitrary"),
        name="norm_matmul",
    )(x, gain.reshape(1, d), w)


def _mix_ffn_kernel(*refs, n_in):
    x_ref = refs[0]
    part_refs, w_refs = refs[1:1 + n_in], refs[1 + n_in:1 + 2 * n_in]
    g_ref, wg_ref, wu_ref, wd_ref, o_ref, h_ref, acc_ref = refs[1 + 2 * n_in:]
    f = pl.program_id(1)

    @pl.when(f == 0)
    def _():
        x = x_ref[...]
        for a_ref, w_ref in zip(part_refs, w_refs):
            x = x + _dot(a_ref[...], w_ref[...])
        h_ref[...] = _rms(x, g_ref[...]).astype(BF16)
        acc_ref[...] = x

    h = h_ref[...]
    act = _silu(_dot(h, wg_ref[...])) * _dot(h, wu_ref[...])
    acc_ref[...] += _dot(act.astype(BF16), wd_ref[...])

    @pl.when(f == pl.num_programs(1) - 1)
    def _():
        o_ref[...] = acc_ref[...]


def _mix_ffn(x, parts, weights, gain, w_gate, w_up, w_down, *, tm, tf):
    t, d = x.shape
    dff = w_gate.shape[1]
    n_in = len(parts)
    in_specs = [pl.BlockSpec((tm, d), lambda i, f: (i, 0))]
    in_specs += [pl.BlockSpec((tm, p.shape[1]), lambda i, f: (i, 0)) for p in parts]
    in_specs += [pl.BlockSpec(w.shape, lambda i, f: (0, 0)) for w in weights]
    in_specs += [pl.BlockSpec((1, d), lambda i, f: (0, 0)),
                 pl.BlockSpec((d, tf), lambda i, f: (0, f)),
                 pl.BlockSpec((d, tf), lambda i, f: (0, f)),
                 pl.BlockSpec((tf, d), lambda i, f: (f, 0))]
    return pl.pallas_call(
        functools.partial(_mix_ffn_kernel, n_in=n_in),
        grid=(t // tm, dff // tf),
        in_specs=in_specs,
        out_specs=pl.BlockSpec((tm, d), lambda i, f: (i, 0)),
        out_shape=jax.ShapeDtypeStruct((t, d), F32),
        scratch_shapes=[pltpu.VMEM((tm, d), BF16), pltpu.VMEM((tm, d), F32)],
        compiler_params=_cparams("parallel", "arbitrary"),
        name="mix_ffn",
    )(x, *parts, *weights, gain.reshape(1, d), w_gate, w_up, w_down)


def _hgrn2_kernel(q_ref, f_ref, v_ref, g_ref, lbl_ref, og_ref, o_ref, state_ref, *,
                  layer_idx, seq, heads, unroll):
    c_len, hd = A_CHUNK, A_EXPAND
    logits = lbl_ref[...]
    e = jnp.exp(logits - jnp.max(logits, axis=0, keepdims=True))
    prob = e / jnp.sum(e, axis=0, keepdims=True)
    lb_all = jnp.sum(prob[:layer_idx + 1], axis=0, keepdims=True) - prob[0:1]
    og_all = og_ref[...]

    sub, n_sub = A_SUB, c_len // A_SUB
    row = lax.broadcasted_iota(jnp.int32, (c_len, hd), 0)
    row_sub = lax.broadcasted_iota(jnp.int32, (sub, hd), 0)
    lane_sub = lax.broadcasted_iota(jnp.int32, (sub, c_len), 1)
    tril = (lax.broadcasted_iota(jnp.int32, (c_len, c_len), 0)
            >= lax.broadcasted_iota(jnp.int32, (c_len, c_len), 1)).astype(BF16)
    ones = jnp.ones((hd, hd), BF16)

    state_ref[...] = jnp.zeros_like(state_ref)

    def step(c, carry):
        chains = [(h, u) for h in range(heads) for u in range(unroll)]
        rows = {u: pl.ds(pl.multiple_of((c * unroll + u) * c_len, c_len), c_len) for u in range(unroll)}
        cols = {h: slice(h * hd, (h + 1) * hd) for h in range(heads)}

        qq, kk, v, b = {}, {}, {}, {}
        for ch in chains:
            h, u = ch
            lb = lb_all[:, cols[h]]
            f = lb + (1.0 - lb) * jax.nn.sigmoid(f_ref[rows[u], cols[h]].astype(F32))
            kk[ch] = 1.0 - f
            qq[ch] = _silu(q_ref[rows[u], cols[h]].astype(F32))
            v[ch] = v_ref[rows[u], cols[h]].astype(BF16)
            hi, mid, lo = _split3(jnp.log(f))
            b[ch] = _dot(tril, hi) + _dot(tril, mid) + _dot(tril, lo)

        r = {}
        for ch in chains:
            pieces = []
            for i in range(n_sub):
                lo_ = i * sub
                for s in range(sub):
                    d = b[ch][lo_:lo_ + sub] - b[ch][lo_ + s:lo_ + s + 1]
                    e_s = jnp.exp(jnp.where(row_sub >= s, d, NEG))
                    pieces.append(qq[ch][lo_:lo_ + sub] * e_s * kk[ch][lo_ + s:lo_ + s + 1])
            r[ch] = _dot(jnp.concatenate(pieces, axis=0).astype(BF16), ones)[:, :c_len]

        o = {}
        for ch in chains:
            blocks = []
            for i in range(n_sub):
                lo_ = i * sub
                sc = jnp.zeros((sub, c_len), F32)
                for s in range(sub):
                    at = (lo_ + s) * sub
                    sc = jnp.where(lane_sub == lo_ + s, r[ch][at:at + sub], sc)
                if i > 0:
                    b_ref_row = b[ch][lo_ - 1:lo_]
                    q_dec = (qq[ch][lo_:lo_ + sub] * jnp.exp(b[ch][lo_:lo_ + sub] - b_ref_row)).astype(BF16)
                    k_grow = (kk[ch] * jnp.exp(jnp.where(row < lo_, b_ref_row - b[ch], NEG))).astype(BF16)
                    sc = sc + _dot_nt(q_dec, k_grow)
                blocks.append(sc)
            o[ch] = _dot(jnp.concatenate(blocks, axis=0).astype(BF16), v[ch])

        upd, q_all = {}, {}
        for ch in chains:
            b_last = b[ch][c_len - 1:c_len]
            k_dec = (kk[ch] * jnp.exp(b_last - b[ch])).astype(BF16)
            upd[ch] = (jnp.exp(b_last), _dot_tn(v[ch], k_dec))
            q_all[ch] = (qq[ch] * jnp.exp(b[ch])).astype(BF16)

        for h in range(heads):
            st = state_ref[h]
            for u in range(unroll):
                ch = (h, u)
                o_c = o[ch] + _dot_nt(q_all[ch], st.astype(BF16))
                st = st * upd[ch][0] + upd[ch][1]
                o_c = o_c * lax.rsqrt(jnp.mean(o_c * o_c, axis=-1, keepdims=True) + EPS)
                gate = _silu(g_ref[rows[u], cols[h]].astype(F32))
                o_ref[rows[u], cols[h]] = (o_c * og_all[:, cols[h]] * gate).astype(BF16)
            state_ref[h] = st
        return carry

    lax.fori_loop(0, seq // (c_len * unroll), step, 0)


def _hgrn2(proj, lb_logits, o_gain, *, layer_idx, batch, seq, heads_per_step, chunks_per_step):
    heads = o_gain.shape[0] // A_EXPAND
    groups = heads // heads_per_step
    width = heads_per_step * A_EXPAND
    n_layers = lb_logits.shape[0]

    def col(off):
        return pl.BlockSpec((seq, width), lambda b, h: (b, h + off * groups))

    return pl.pallas_call(
        functools.partial(_hgrn2_kernel, layer_idx=layer_idx, seq=seq, heads=heads_per_step,
                          unroll=chunks_per_step),
        grid=(batch, groups),
        in_specs=[col(0), col(1), col(2), col(3),
                  pl.BlockSpec((n_layers, width), lambda b, h: (0, h)),
                  pl.BlockSpec((1, width), lambda b, h: (0, h))],
        out_specs=pl.BlockSpec((seq, width), lambda b, h: (b, h)),
        out_shape=jax.ShapeDtypeStruct((batch * seq, heads * A_EXPAND), BF16),
        scratch_shapes=[pltpu.VMEM((heads_per_step, A_EXPAND, A_EXPAND), F32)],
        compiler_params=_cparams("parallel", "parallel"),
        name="hgrn2",
    )(proj, proj, proj, proj, lb_logits, o_gain.reshape(1, -1))


DIL_NEAR = DILATED_PATTERNS[:2]
DIL_FAR = DILATED_PATTERNS[2]
DIL_BACK = max(w for w, _ in DIL_NEAR)


def _dilated_near_log_weights():
    r = np.arange(ATT_TILE)[:, None]
    c = np.arange(DIL_BACK + ATT_TILE)[None, :]
    d = r + DIL_BACK - c
    mult = np.zeros_like(d)
    for window, dilation in DIL_NEAR:
        mult += (d >= 0) & (d % dilation == 0) & (d <= window)
    return np.where(mult > 0, np.log2(np.maximum(mult, 1)), NEG).astype(np.float32)


FAR_MAX_LANE = 96
LOG2E = 1.4426950408889634
FAR_UNROLL = 2
NEAR_UNROLL = 2
POS_SHIFT = 7
POS_RADIX = 1 << POS_SHIFT


def _tile_rows(first_tile, n_tiles):
    start = first_tile * ATT_TILE
    if not isinstance(first_tile, int):
        start = pl.multiple_of(start, ATT_TILE)
    return pl.ds(start, n_tiles * ATT_TILE)


def _softmax_blocks(qs, ks, biases, vs):
    return _softmax_of([_dot_nt(q, k) + bias for q, k, bias in zip(qs, ks, biases)], vs)


def _softmax_of(a, vs):
    m = [jnp.max(x, axis=-1, keepdims=True) for x in a]
    p = [jnp.exp2(x - mm) for x, mm in zip(a, m)]
    acc = [_dot(x.astype(BF16), v) for x, v in zip(p, vs)]
    return list(zip(m, acc))


def _online_softmax_of(a, v_tile, states):
    m_new = [jnp.maximum(st[0], jnp.max(x, axis=-1, keepdims=True)) for x, st in zip(a, states)]
    alpha = [jnp.exp2(st[0] - mn) for st, mn in zip(states, m_new)]
    p = [jnp.exp2(x - mn) for x, mn in zip(a, m_new)]
    pv = [_dot(x.astype(BF16), v_tile) for x in p]
    return tuple((mn, al * st[1] + o) for mn, al, st, o in zip(m_new, alpha, states, pv))


def _normalised(acc):
    return acc[:, :HEAD_DIM] / acc[:, HEAD_DIM:HEAD_DIM + 1]


def _with_ones(v):
    lane = lax.broadcasted_iota(jnp.int32, v.shape, 1)
    return jnp.concatenate([v, jnp.where(lane == 0, 1.0, 0.0)], axis=1)


def _slope_pieces(slopes):
    hi = slopes.astype(BF16).astype(F32)
    mid = (slopes - hi).astype(BF16).astype(F32)
    lo = (slopes - hi - mid).astype(BF16).astype(F32)
    return jnp.stack([hi, mid, lo], axis=1).reshape(-1)


def _alibi_query_cols(pieces, rows):
    lane = lax.broadcasted_iota(jnp.int32, (rows, HEAD_DIM), 1)
    out = jnp.zeros((rows, HEAD_DIM), F32)
    for n, piece in enumerate(pieces):
        out = jnp.where(lane == n, piece * POS_RADIX, jnp.where(lane == n + 3, piece, out))
    return out


def _alibi_key_cols(pos):
    lane = lax.broadcasted_iota(jnp.int32, pos.shape, 1)
    hi = lax.shift_right_logical(pos, POS_SHIFT).astype(F32)
    lo = jnp.bitwise_and(pos, POS_RADIX - 1).astype(F32)
    return jnp.where(lane < 3, hi, jnp.where(lane < 6, lo, 0.0))


def _dilated_kernel(pieces_ref, q_ref, k_ref, v_ref, qg_ref, kg_ref, lw_ref, o_ref,
                    qn_ref, kn_ref, vb_ref, qr_ref, kr_ref, vr_ref, n32_ref, far_ref, *, seq):
    tile = ATT_TILE
    pair = pl.program_id(1)
    n_tiles = seq // tile
    back = DIL_BACK // tile
    stride = DIL_FAR[1]
    n_res = seq // stride
    far_mask = jnp.where(lax.broadcasted_iota(jnp.int32, (n_res, n_res), 0)
                         >= lax.broadcasted_iota(jnp.int32, (n_res, n_res), 1), 0.0, NEG)

    pos = lax.broadcasted_iota(jnp.int32, (seq, HEAD_DIM), 0)

    def copies(hh, values, nat_ref, res_ref):
        n32_ref[...] = values
        nat_ref[hh] = values.astype(BF16)
        for r in range(stride):
            res_ref[hh, r * n_res:(r + 1) * n_res, :] = n32_ref[pl.ds(r, n_res, stride=stride), :].astype(BF16)

    for hh in range(2):
        lanes = slice(hh * HEAD_DIM, (hh + 1) * HEAD_DIM)
        pieces = [pieces_ref[(2 * pair + hh) * 3 + n] for n in range(3)]
        q_aug = jnp.concatenate([_rms(q_ref[:, lanes].astype(F32), qg_ref[...]) * (ATTN_SCALE * LOG2E),
                                 _alibi_query_cols(pieces, seq)], axis=1)
        k_aug = jnp.concatenate([_rms(k_ref[:, lanes].astype(F32), kg_ref[...]), _alibi_key_cols(pos)], axis=1)
        copies(hh, q_aug, qn_ref, qr_ref)
        copies(hh, k_aug, kn_ref, kr_ref)
        copies(hh, _with_ones(v_ref[:, lanes].astype(F32)), vb_ref, vr_ref)

    def far_step(step, carry):
        res = [FAR_UNROLL * step + n for n in range(FAR_UNROLL)]
        units = [(hh, n) for n in range(FAR_UNROLL) for hh in range(2)]
        rows = [pl.ds(pl.multiple_of(r * n_res, n_res), n_res) for r in res]
        parts = _softmax_blocks([qr_ref[hh, rows[n], :] for hh, n in units],
                                [kr_ref[hh, rows[n], :] for hh, n in units],
                                [far_mask] * len(units),
                                [vr_ref[hh, rows[n], :] for hh, n in units])
        for (hh, n), (m, acc) in zip(units, parts):
            far_ref[hh, pl.ds(res[n], n_res, stride=stride), :] = jnp.concatenate(
                [acc[:, :FAR_MAX_LANE], jnp.broadcast_to(m, (n_res, 2 * HEAD_DIM - FAR_MAX_LANE))], axis=1)
        return carry

    lax.fori_loop(0, stride // FAR_UNROLL, far_step, 0)

    def near_tiles(tiles):
        units = [(hh, t) for t in range(len(tiles)) for hh in range(2)]
        rows = [_tile_rows(i, 1) for i, _, _ in tiles]
        krows = [_tile_rows(first, n_win) for _, first, n_win in tiles]
        bias = [lw_ref[:, (back + 1 - n_win) * tile:] for _, _, n_win in tiles]
        parts = _softmax_blocks([qn_ref[hh, rows[t], :] for hh, t in units],
                                [kn_ref[hh, krows[t], :] for hh, t in units],
                                [bias[t] for hh, t in units],
                                [vb_ref[hh, krows[t], :] for hh, t in units])
        outs = {}
        for (hh, t), (m1, acc1) in zip(units, parts):
            far = far_ref[hh, rows[t], :]
            m2 = far[:, FAR_MAX_LANE:FAR_MAX_LANE + 1]
            m = jnp.maximum(m1, m2)
            outs[hh, t] = _normalised(jnp.exp2(m1 - m) * acc1 + jnp.exp2(m2 - m) * far)
        for t in range(len(tiles)):
            o_ref[rows[t], :] = jnp.concatenate([outs[0, t], outs[1, t]], axis=1).astype(BF16)

    head_tiles = min(back, n_tiles)
    near_tiles([(i, 0, i + 1) for i in range(head_tiles)])

    def near_body(step, carry):
        first = head_tiles + NEAR_UNROLL * step
        near_tiles([(first + n, first + n - back, back + 1) for n in range(NEAR_UNROLL)])
        return carry

    lax.fori_loop(0, (n_tiles - head_tiles) // NEAR_UNROLL, near_body, 0)


def _dilated(proj, q_gain, k_gain, *, batch, seq, col0):
    pairs = 4
    heads = 2 * pairs
    slopes = (2.0 ** (-8.0 * jnp.arange(1, heads + 1, dtype=F32) / heads)).astype(F32)
    assert seq <= DIL_FAR[0] and seq % (DIL_FAR[1] * 8) == 0 and seq % ATT_TILE == 0
    assert seq <= POS_RADIX * 256 and DIL_FAR[1] % FAR_UNROLL == 0
    assert (seq // ATT_TILE - min(DIL_BACK // ATT_TILE, seq // ATT_TILE)) % NEAR_UNROLL == 0
    win = DIL_BACK + ATT_TILE

    def col(off):
        return pl.BlockSpec((seq, 2 * HEAD_DIM), lambda b, p: (b, col0 + off * pairs + p))

    head_copy = pltpu.VMEM((2, seq, 2 * HEAD_DIM), BF16)
    return pl.pallas_call(
        functools.partial(_dilated_kernel, seq=seq),
        grid=(batch, pairs),
        in_specs=[pl.BlockSpec(memory_space=pltpu.SMEM),
                  col(0), col(1), col(2),
                  pl.BlockSpec((1, HEAD_DIM), lambda b, p: (0, 0)),
                  pl.BlockSpec((1, HEAD_DIM), lambda b, p: (0, 0)),
                  pl.BlockSpec((ATT_TILE, win), lambda b, p: (0, 0))],
        out_specs=pl.BlockSpec((seq, 2 * HEAD_DIM), lambda b, p: (b, p)),
        out_shape=jax.ShapeDtypeStruct((batch * seq, heads * HEAD_DIM), BF16),
        scratch_shapes=[head_copy] * 6 + [pltpu.VMEM((seq, 2 * HEAD_DIM), F32),
                                          pltpu.VMEM((2, seq, 2 * HEAD_DIM), F32)],
        compiler_params=_cparams("parallel", "parallel"),
        name="dilated",
    )(_slope_pieces(slopes * LOG2E), proj, proj, proj, q_gain.reshape(1, -1), k_gain.reshape(1, -1),
      jnp.asarray(_dilated_near_log_weights()))


def _nsa_overlap_t(n_blocks, n_cmp_rows):
    blk = np.arange(n_blocks)[:, None]
    start = np.arange(n_cmp_rows)[None, :] * CMP_STRIDE
    return ((start < (blk + 1) * SLC_BLOCK) & (start + CMP_LEN > blk * SLC_BLOCK)).astype(np.float32)


def _nsa_kernel(pieces_ref, x_ref, qg_ref, kg_ref, pe_ref, w1_ref, w2_ref, ovl_ref, o_ref,
                qn_ref, ks_ref, vs_ref, kw_ref, vw_ref, kc_ref, vc_ref, rc_slc_ref, rc_win_ref, *, seq):
    tile = NSA_TILE
    n_tiles = seq // tile
    n_cmp = seq // CMP_STRIDE
    n_blk = seq // SLC_BLOCK
    grp = pl.program_id(1)
    heads = range(C_GROUP)

    prep_rows = min(NSA_PREP_ROWS, seq)
    q_cols = [_alibi_query_cols([pieces_ref[(grp * C_GROUP + j) * 3 + n] for n in range(3)], prep_rows)
              for j in heads]
    row_pos = lax.broadcasted_iota(jnp.int32, (prep_rows, HEAD_DIM), 0)

    def prep(i, carry):
        rows = pl.ds(pl.multiple_of(i * prep_rows, prep_rows), prep_rows)
        k_cols = _alibi_key_cols(i * prep_rows + row_pos)
        x = x_ref[rows, :NSA_GATE_COL].astype(F32)
        for j in heads:
            qj = _rms(x[:, j * HEAD_DIM:(j + 1) * HEAD_DIM], qg_ref[...]) * (ATTN_SCALE * LOG2E)
            qj = jnp.concatenate([qj, q_cols[j]], axis=1).astype(BF16)
            for t in range(prep_rows // tile):
                at = ((i * (prep_rows // tile) + t) * C_GROUP + j) * tile
                qn_ref[pl.ds(pl.multiple_of(at, tile), tile), :] = qj[t * tile:(t + 1) * tile]
        kc_ref[rows, :] = x[:, 256:320]
        vc_ref[rows, :] = x[:, 320:384]
        ks_ref[rows, :] = jnp.concatenate([_rms(x[:, 384:448], kg_ref[1:2]), k_cols], axis=1).astype(BF16)
        vs_ref[rows, :] = _with_ones(x[:, 448:512]).astype(BF16)
        kw_ref[rows, :] = jnp.concatenate([_rms(x[:, 512:576], kg_ref[2:3]), k_cols], axis=1).astype(BF16)
        vw_ref[rows, :] = _with_ones(x[:, 576:640]).astype(BF16)
        return carry

    lax.fori_loop(0, seq // prep_rows, prep, 0)
    kc_ref[seq:, :] = jnp.zeros((CMP_LEN, HEAD_DIM), F32)
    vc_ref[seq:, :] = jnp.zeros((CMP_LEN, HEAD_DIM), F32)

    def compress(src_ref, which):
        hid = jnp.zeros((n_cmp, CMP_HIDDEN), F32)
        for pos in range(CMP_LEN):
            tok = src_ref[pl.ds(pos, n_cmp, stride=CMP_STRIDE), :] + pe_ref[which, pos:pos + 1, :]
            hid = hid + _dot(tok.astype(BF16), w1_ref[which, pos * HEAD_DIM:(pos + 1) * HEAD_DIM, :])
        return _dot(_silu(hid).astype(BF16), w2_ref[which])

    cmp_pos = lax.broadcasted_iota(jnp.int32, (n_cmp, HEAD_DIM), 0) * CMP_STRIDE + (CMP_LEN - 1)
    k_cmp = jnp.concatenate([_rms(compress(kc_ref, 0), kg_ref[0:1]), _alibi_key_cols(cmp_pos)],
                            axis=1).astype(BF16)
    v_cmp = compress(vc_ref, 1).astype(BF16)
    ovl = ovl_ref[...].astype(BF16)

    rc_slc_ref[...] = (lax.broadcasted_iota(jnp.int32, (tile, SLC_CHUNK), 0)
                       - lax.broadcasted_iota(jnp.int32, (tile, SLC_CHUNK), 1)).astype(F32)
    rc_win_ref[...] = (lax.broadcasted_iota(jnp.int32, (tile, WIN + tile), 0)
                       - lax.broadcasted_iota(jnp.int32, (tile, WIN + tile), 1)).astype(F32)
    blk_row = lax.broadcasted_iota(jnp.int32, (tile, SLC_CHUNK), 0)
    blk_of_key = lax.broadcasted_iota(jnp.int32, (tile, SLC_CHUNK), 1) // SLC_BLOCK
    cmp_end = (lax.broadcasted_iota(jnp.int32, (tile, n_cmp), 1) * CMP_STRIDE + (CMP_LEN - 1))
    blk_i = lax.broadcasted_iota(jnp.int32, (n_blk, tile), 0)
    tok_lane = lax.broadcasted_iota(jnp.int32, (n_blk, tile), 1)

    def head_rows(s):
        return [s[j * tile:(j + 1) * tile] for j in heads]

    def q_tile(i, carry):
        t0 = i * tile
        rows = pl.ds(pl.multiple_of(t0, tile), tile)
        q4 = qn_ref[pl.ds(pl.multiple_of(i * C_GROUP * tile, tile), C_GROUP * tile), :]

        vis_c = (t0 + lax.broadcasted_iota(jnp.int32, (tile, n_cmp), 0)) >= cmp_end
        a_c = [jnp.where(vis_c, x, NEG) for x in head_rows(_dot_nt(q4, k_cmp))]
        m_c = [jnp.max(x, axis=-1, keepdims=True) for x in a_c]
        e_c = [jnp.where(vis_c, jnp.exp2(x - m), 0.0) for x, m in zip(a_c, m_c)]
        l_c = [jnp.sum(x, axis=-1, keepdims=True) for x in e_c]
        p_c = [x / jnp.where(l > 0, l, 1.0) for x, l in zip(e_c, l_c)]
        o_cmp = [_dot(x.astype(BF16), v_cmp) for x in p_c]
        hi, mid, lo = _split3(p_c[0] + p_c[1] + p_c[2] + p_c[3])
        imp = _dot_nt(ovl, hi) + _dot_nt(ovl, mid) + _dot_nt(ovl, lo)

        cur = lax.shift_right_logical(t0 + tok_lane, 6)
        forced = (blk_i == 0) | (blk_i == cur) | (blk_i == cur - 1)
        score = jnp.where(blk_i > cur, -1.0, jnp.where(forced, 1e30, imp))
        rank = jnp.zeros((n_blk, tile), F32)
        for other in range(n_blk):
            s_o = score[other:other + 1, :]
            rank = rank + jnp.where(s_o > score, 1.0,
                                    jnp.where(s_o == score, jnp.where(blk_i > other, 1.0, 0.0), 0.0))
        sel = jnp.where(rank < SLC_TOPN, jnp.where(score >= 0, 1.0, 0.0), 0.0)
        sel_t = jnp.concatenate([sel, jnp.zeros((tile - n_blk, tile), F32)], axis=0).T.astype(BF16)

        def slc_step(c, st):
            krows = pl.ds(pl.multiple_of(c * SLC_CHUNK, SLC_CHUNK), SLC_CHUNK)
            expand = jnp.where(blk_row == c * (SLC_CHUNK // SLC_BLOCK) + blk_of_key, 1.0, 0.0).astype(BF16)
            picked = _dot(sel_t, expand)
            dist = rc_slc_ref[...] + jnp.asarray(t0 - c * SLC_CHUNK, F32)
            mask = jnp.where(picked > 0.5, jnp.where(dist >= 0, 0.0, NEG), NEG)
            a = [x + mask for x in head_rows(_dot_nt(q4, ks_ref[krows, :]))]
            return _online_softmax_of(a, vs_ref[krows, :], st)

        init = tuple((jnp.full((tile, 1), NEG, F32), jnp.zeros((tile, 2 * HEAD_DIM), F32)) for _ in heads)
        slc = lax.fori_loop(0, (t0 + tile - 1) // SLC_CHUNK + 1, slc_step, init)

        first = jnp.maximum(i - WIN // tile, 0)
        krows = pl.ds(pl.multiple_of(first * tile, tile), WIN + tile)
        dist = rc_win_ref[...] + jnp.asarray((i - first) * tile, F32)
        mask = jnp.where(dist >= 0, jnp.where(dist < WIN, 0.0, NEG), NEG)
        win = _softmax_of([x + mask for x in head_rows(_dot_nt(q4, kw_ref[krows, :]))],
                          [vw_ref[krows, :]] * C_GROUP)

        gate = jax.nn.sigmoid(x_ref[rows, NSA_GATE_COL:NSA_GATE_COL + 128].astype(F32))
        outs = []
        for j in heads:
            outs.append(gate[:, j:j + 1] * o_cmp[j]
                        + gate[:, C_GROUP + j:C_GROUP + j + 1] * _normalised(slc[j][1])
                        + gate[:, 2 * C_GROUP + j:2 * C_GROUP + j + 1] * _normalised(win[j][1]))
        o_ref[rows, :] = jnp.concatenate(outs, axis=1).astype(BF16)
        return carry

    lax.fori_loop(0, n_tiles, q_tile, 0)


def _nsa(proj, q_gain, k_gain, cmp_pe, cmp_w1, cmp_w2, *, batch, seq):
    heads = C_KV_HEADS * C_GROUP
    slopes = (2.0 ** (-8.0 * jnp.arange(1, heads + 1, dtype=F32) / heads)).astype(F32)
    n_cmp = seq // CMP_STRIDE
    n_blk = seq // SLC_BLOCK
    assert seq % SLC_CHUNK == 0 and seq >= WIN + NSA_TILE and n_blk <= NSA_TILE and n_cmp % 128 == 0
    assert SLC_CHUNK % NSA_TILE == 0 and WIN % NSA_TILE == 0
    full = lambda *shape: pl.BlockSpec(shape, lambda b, g: (0,) * len(shape))
    head_copy = pltpu.VMEM((seq, 2 * HEAD_DIM), BF16)
    return pl.pallas_call(
        functools.partial(_nsa_kernel, seq=seq),
        grid=(batch, C_KV_HEADS),
        in_specs=[pl.BlockSpec(memory_space=pltpu.SMEM),
                  pl.BlockSpec((seq, NSA_GROUP_COLS), lambda b, g: (b, g)),
                  full(1, HEAD_DIM), full(3, HEAD_DIM), full(2, CMP_LEN, HEAD_DIM),
                  full(2, CMP_LEN * HEAD_DIM, CMP_HIDDEN), full(2, CMP_HIDDEN, HEAD_DIM),
                  full(n_blk, n_cmp)],
        out_specs=pl.BlockSpec((seq, C_GROUP * HEAD_DIM), lambda b, g: (b, g)),
        out_shape=jax.ShapeDtypeStruct((batch * seq, heads * HEAD_DIM), BF16),
        scratch_shapes=[pltpu.VMEM((seq * C_GROUP, 2 * HEAD_DIM), BF16)] + [head_copy] * 4 + [
            pltpu.VMEM((seq + CMP_LEN, HEAD_DIM), F32),
            pltpu.VMEM((seq + CMP_LEN, HEAD_DIM), F32),
            pltpu.VMEM((NSA_TILE, SLC_CHUNK), F32),
            pltpu.VMEM((NSA_TILE, WIN + NSA_TILE), F32)],
        compiler_params=_cparams("parallel", "parallel"),
        name="nsa",
    )(_slope_pieces(slopes * LOG2E), proj, q_gain.reshape(1, -1), k_gain, cmp_pe,
      cmp_w1.astype(BF16), cmp_w2.astype(BF16), jnp.asarray(_nsa_overlap_t(n_blk, n_cmp)))


def _nsa_in_weight(w_in, d_model):
    kv_w = C_KV_HEADS * HEAD_DIM
    cols = []
    for g in range(C_KV_HEADS):
        idx = list(range(g * C_GROUP * HEAD_DIM, (g + 1) * C_GROUP * HEAD_DIM))
        for part in range(6):
            base = d_model + part * kv_w + g * HEAD_DIM
            idx += list(range(base, base + HEAD_DIM))
        gate0 = d_model + 6 * kv_w
        idx += [gate0 + n * C_KV_HEADS * C_GROUP + g * C_GROUP + j for n in range(3) for j in range(C_GROUP)]
        cols.append(jnp.pad(w_in[:, np.asarray(idx)], ((0, 0), (0, NSA_GROUP_COLS - len(idx)))))
    return jnp.concatenate(cols, axis=1)


def kernel(x, attn_norm, ffn_norm, hy_w_in, hy_lb_logits, hy_o_gain, hy_q_gain, hy_k_gain, hy_w_out, nsa_w_in, nsa_q_gain, nsa_k_gain, nsa_cmp_pe, nsa_cmp_w1, nsa_cmp_w2, nsa_w_out, ffn_w_gate, ffn_w_up, ffn_w_down):
    batch, seq, d_model = x.shape
    depth = attn_norm.shape[0]
    a_width = hy_o_gain.shape[1]
    dff = ffn_w_gate.shape[2]
    h = x.reshape(batch * seq, d_model)
    for layer in range(depth):
        i = layer // 2
        if layer % 2 == 0:
            proj = _norm_matmul(h, attn_norm[layer], hy_w_in[i].astype(BF16), tm=1024,
                                tn=hy_w_in.shape[2] // 2)
            a_out = _hgrn2(proj, hy_lb_logits, hy_o_gain[i], layer_idx=i, batch=batch, seq=seq,
                           heads_per_step=4, chunks_per_step=2)
            b_out = _dilated(proj, hy_q_gain[i], hy_k_gain[i], batch=batch, seq=seq,
                             col0=4 * a_width // (2 * HEAD_DIM))
            w_out = hy_w_out[i].astype(BF16)
            mixed, w_mix = [a_out, b_out], [w_out[:a_width], w_out[a_width:]]
        else:
            proj = _norm_matmul(h, attn_norm[layer], _nsa_in_weight(nsa_w_in[i], d_model).astype(BF16),
                                tm=1024, tn=2 * NSA_GROUP_COLS)
            o = _nsa(proj, nsa_q_gain[i], nsa_k_gain[i], nsa_cmp_pe[i], nsa_cmp_w1[i], nsa_cmp_w2[i],
                     batch=batch, seq=seq)
            mixed, w_mix = [o], [nsa_w_out[i].astype(BF16)]
        h = _mix_ffn(h, mixed, w_mix, ffn_norm[layer], ffn_w_gate[layer].astype(BF16),
                     ffn_w_up[layer].astype(BF16), ffn_w_down[layer].astype(BF16), tm=512, tf=dff // 2)
    return h.reshape(batch, seq, d_model)
```

```python
import functools

import numpy as np
import jax
import jax.numpy as jnp
from jax import lax
from jax.experimental import pallas as pl
from jax.experimental.pallas import tpu as pltpu

F32 = jnp.float32
BF16 = jnp.bfloat16

EPS = 1e-6
HEAD_DIM = 64
ATTN_SCALE = HEAD_DIM ** -0.5
NEG = -1e30

A_EXPAND = 128
A_CHUNK = 32
A_SUB = 8

DILATED_PATTERNS = ((128, 1), (512, 4), (2048, 16))
ATT_TILE = 128

C_KV_HEADS = 4
C_GROUP = 4
CMP_LEN = 32
CMP_STRIDE = 16
CMP_HIDDEN = 2 * HEAD_DIM
SLC_BLOCK = 64
SLC_TOPN = 8
SLC_CHUNK = 512
WIN = 512
NSA_GROUP_COLS = 768
NSA_GATE_COL = 640
NSA_TILE = 256
NSA_PREP_ROWS = 2048

V7X_VMEM_LIMIT = 56 * 1024 * 1024


def _cparams(*sem):
    return pltpu.CompilerParams(dimension_semantics=sem, vmem_limit_bytes=V7X_VMEM_LIMIT)


def _rms(x, gain):
    return x * lax.rsqrt(jnp.mean(x * x, axis=-1, keepdims=True) + EPS) * gain


def _silu(x):
    return x * jax.nn.sigmoid(x)


def _split3(x):
    hi = x.astype(BF16)
    r1 = x - hi.astype(F32)
    mid = r1.astype(BF16)
    lo = (r1 - mid.astype(F32)).astype(BF16)
    return hi, mid, lo


def _dot(a, b):
    return jnp.dot(a, b, preferred_element_type=F32)


def _dot_nt(a, b):
    return lax.dot_general(a, b, (((1,), (1,)), ((), ())), preferred_element_type=F32)


def _dot_tn(a, b):
    return lax.dot_general(a, b, (((0,), (0,)), ((), ())), preferred_element_type=F32)


def _norm_matmul_kernel(x_ref, g_ref, w_ref, o_ref, h_ref):
    @pl.when(pl.program_id(1) == 0)
    def _():
        h_ref[...] = _rms(x_ref[...], g_ref[...]).astype(BF16)

    o_ref[...] = _dot(h_ref[...], w_ref[...]).astype(o_ref.dtype)


def _norm_matmul(x, gain, w, *, tm, tn):
    t, d = x.shape
    n = w.shape[1]
    return pl.pallas_call(
        _norm_matmul_kernel,
        grid=(t // tm, n // tn),
        in_specs=[pl.BlockSpec((tm, d), lambda i, j: (i, 0)),
                  pl.BlockSpec((1, d), lambda i, j: (0, 0)),
                  pl.BlockSpec((d, tn), lambda i, j: (0, j))],
        out_specs=pl.BlockSpec((tm, tn), lambda i, j: (i, j)),
        out_shape=jax.ShapeDtypeStruct((t, n), BF16),
        scratch_shapes=[pltpu.VMEM((tm, d), BF16)],
        compiler_params=_cparams("parallel", "arbitrary"),
        name="norm_matmul",
    )(x, gain.reshape(1, d), w)


def _lane_tile_halves(width):
    mid = (width // 256) * 128 if width >= 256 else width
    return [(0, mid), (mid, width)] if mid < width else [(0, width)]


def _mix_ffn_kernel(*refs, n_in):
    x_ref = refs[0]
    part_refs, w_refs = refs[1:1 + n_in], refs[1 + n_in:1 + 2 * n_in]
    g_ref, wg_ref, wu_ref, wd_ref, o_ref, h_ref = refs[1 + 2 * n_in:]

    @pl.when(pl.program_id(1) == 0)
    def _():
        x = x_ref[...]
        for a_ref, w_ref in zip(part_refs, w_refs):
            x = x + _dot(a_ref[...], w_ref[...])
        h_ref[...] = _rms(x, g_ref[...]).astype(BF16)
        o_ref[...] = x

    h = h_ref[...]
    tf = wg_ref.shape[1]
    for lo_, hi_ in _lane_tile_halves(tf):
        act = _silu(_dot(h, wg_ref[:, lo_:hi_])) * _dot(h, wu_ref[:, lo_:hi_])
        o_ref[...] += _dot(act.astype(BF16), wd_ref[lo_:hi_, :])


def _mix_ffn(x, parts, weights, gain, w_gate, w_up, w_down, *, tm, tf):
    t, d = x.shape
    dff = w_gate.shape[1]
    n_in = len(parts)
    in_specs = [pl.BlockSpec((tm, d), lambda i, f: (i, 0))]
    in_specs += [pl.BlockSpec((tm, p.shape[1]), lambda i, f: (i, 0)) for p in parts]
    in_specs += [pl.BlockSpec(w.shape, lambda i, f: (0, 0)) for w in weights]
    in_specs += [pl.BlockSpec((1, d), lambda i, f: (0, 0)),
                 pl.BlockSpec((d, tf), lambda i, f: (0, f)),
                 pl.BlockSpec((d, tf), lambda i, f: (0, f)),
                 pl.BlockSpec((tf, d), lambda i, f: (f, 0))]
    return pl.pallas_call(
        functools.partial(_mix_ffn_kernel, n_in=n_in),
        grid=(t // tm, dff // tf),
        in_specs=in_specs,
        out_specs=pl.BlockSpec((tm, d), lambda i, f: (i, 0)),
        out_shape=jax.ShapeDtypeStruct((t, d), F32),
        scratch_shapes=[pltpu.VMEM((tm, d), BF16)],
        compiler_params=_cparams("parallel", "arbitrary"),
        name="mix_ffn",
    )(x, *parts, *weights, gain.reshape(1, d), w_gate, w_up, w_down)


def _hgrn2_kernel(q_ref, f_ref, v_ref, g_ref, lbl_ref, og_ref, o_ref, state_ref, *,
                  layer_idx, seq, heads, unroll):
    c_len, hd = A_CHUNK, A_EXPAND
    logits = lbl_ref[...]
    e = jnp.exp(logits - jnp.max(logits, axis=0, keepdims=True))
    prob = e / jnp.sum(e, axis=0, keepdims=True)
    lb_all = jnp.sum(prob[:layer_idx + 1], axis=0, keepdims=True) - prob[0:1]
    og_all = og_ref[...]

    sub, n_sub = A_SUB, c_len // A_SUB
    row = lax.broadcasted_iota(jnp.int32, (c_len, hd), 0)
    row_sub = lax.broadcasted_iota(jnp.int32, (sub, hd), 0)
    lane_sub = lax.broadcasted_iota(jnp.int32, (sub, c_len), 1)
    tril = (lax.broadcasted_iota(jnp.int32, (c_len, c_len), 0)
            >= lax.broadcasted_iota(jnp.int32, (c_len, c_len), 1)).astype(BF16)
    ones = jnp.ones((hd, hd), BF16)

    state_ref[...] = jnp.zeros_like(state_ref)

    def step(c, carry):
        chains = [(h, u) for h in range(heads) for u in range(unroll)]
        rows = {u: pl.ds(pl.multiple_of((c * unroll + u) * c_len, c_len), c_len) for u in range(unroll)}
        cols = {h: slice(h * hd, (h + 1) * hd) for h in range(heads)}

        qq, kk, v, b = {}, {}, {}, {}
        for ch in chains:
            h, u = ch
            lb = lb_all[:, cols[h]]
            f = lb + (1.0 - lb) * jax.nn.sigmoid(f_ref[rows[u], cols[h]].astype(F32))
            kk[ch] = 1.0 - f
            qq[ch] = _silu(q_ref[rows[u], cols[h]].astype(F32))
            v[ch] = v_ref[rows[u], cols[h]].astype(BF16)
            hi, mid, lo = _split3(jnp.log(f))
            b[ch] = _dot(tril, hi) + _dot(tril, mid) + _dot(tril, lo)

        r = {}
        for ch in chains:
            pieces = []
            for i in range(n_sub):
                lo_ = i * sub
                for s in range(sub):
                    d = b[ch][lo_:lo_ + sub] - b[ch][lo_ + s:lo_ + s + 1]
                    e_s = jnp.exp(jnp.where(row_sub >= s, d, NEG))
                    pieces.append(qq[ch][lo_:lo_ + sub] * e_s * kk[ch][lo_ + s:lo_ + s + 1])
            r[ch] = _dot(jnp.concatenate(pieces, axis=0).astype(BF16), ones)[:, :c_len]

        o = {}
        for ch in chains:
            blocks = []
            for i in range(n_sub):
                lo_ = i * sub
                sc = jnp.zeros((sub, c_len), F32)
                for s in range(sub):
                    at = (lo_ + s) * sub
                    sc = jnp.where(lane_sub == lo_ + s, r[ch][at:at + sub], sc)
                if i > 0:
                    b_ref_row = b[ch][lo_ - 1:lo_]
                    q_dec = (qq[ch][lo_:lo_ + sub] * jnp.exp(b[ch][lo_:lo_ + sub] - b_ref_row)).astype(BF16)
                    k_grow = (kk[ch] * jnp.exp(jnp.where(row < lo_, b_ref_row - b[ch], NEG))).astype(BF16)
                    sc = sc + _dot_nt(q_dec, k_grow)
                blocks.append(sc)
            o[ch] = _dot(jnp.concatenate(blocks, axis=0).astype(BF16), v[ch])

        upd, q_all = {}, {}
        for ch in chains:
            b_last = b[ch][c_len - 1:c_len]
            k_dec = (kk[ch] * jnp.exp(b_last - b[ch])).astype(BF16)
            upd[ch] = (jnp.exp(b_last), _dot_tn(v[ch], k_dec))
            q_all[ch] = (qq[ch] * jnp.exp(b[ch])).astype(BF16)

        for h in range(heads):
            st = state_ref[h]
            for u in range(unroll):
                ch = (h, u)
                o_c = o[ch] + _dot_nt(q_all[ch], st.astype(BF16))
                st = st * upd[ch][0] + upd[ch][1]
                o_c = o_c * lax.rsqrt(jnp.mean(o_c * o_c, axis=-1, keepdims=True) + EPS)
                gate = _silu(g_ref[rows[u], cols[h]].astype(F32))
                o_ref[rows[u], cols[h]] = (o_c * og_all[:, cols[h]] * gate).astype(BF16)
            state_ref[h] = st
        return carry

    lax.fori_loop(0, seq // (c_len * unroll), step, 0)


def _hgrn2(proj, lb_logits, o_gain, *, layer_idx, batch, seq, heads_per_step, chunks_per_step):
    heads = o_gain.shape[0] // A_EXPAND
    groups = heads // heads_per_step
    width = heads_per_step * A_EXPAND
    n_layers = lb_logits.shape[0]

    def col(off):
        return pl.BlockSpec((seq, width), lambda b, h: (b, h + off * groups))

    return pl.pallas_call(
        functools.partial(_hgrn2_kernel, layer_idx=layer_idx, seq=seq, heads=heads_per_step,
                          unroll=chunks_per_step),
        grid=(batch, groups),
        in_specs=[col(0), col(1), col(2), col(3),
                  pl.BlockSpec((n_layers, width), lambda b, h: (0, h)),
                  pl.BlockSpec((1, width), lambda b, h: (0, h))],
        out_specs=pl.BlockSpec((seq, width), lambda b, h: (b, h)),
        out_shape=jax.ShapeDtypeStruct((batch * seq, heads * A_EXPAND), BF16),
        scratch_shapes=[pltpu.VMEM((heads_per_step, A_EXPAND, A_EXPAND), F32)],
        compiler_params=_cparams("parallel", "parallel"),
        name="hgrn2",
    )(proj, proj, proj, proj, lb_logits, o_gain.reshape(1, -1))


DIL_NEAR = DILATED_PATTERNS[:2]
DIL_FAR = DILATED_PATTERNS[2]
DIL_BACK = max(w for w, _ in DIL_NEAR)


def _dilated_near_log_weights():
    r = np.arange(ATT_TILE)[:, None]
    c = np.arange(DIL_BACK + ATT_TILE)[None, :]
    d = r + DIL_BACK - c
    mult = np.zeros_like(d)
    for window, dilation in DIL_NEAR:
        mult += (d >= 0) & (d % dilation == 0) & (d <= window)
    return np.where(mult > 0, np.log2(np.maximum(mult, 1)), NEG).astype(np.float32)


FAR_MAX_LANE = 96
LOG2E = 1.4426950408889634
FAR_UNROLL = 2
NEAR_UNROLL = 2
POS_SHIFT = 7
POS_RADIX = 1 << POS_SHIFT


def _tile_rows(first_tile, n_tiles):
    start = first_tile * ATT_TILE
    if not isinstance(first_tile, int):
        start = pl.multiple_of(start, ATT_TILE)
    return pl.ds(start, n_tiles * ATT_TILE)


def _softmax_blocks(qs, ks, biases, vs):
    return _softmax_of([_dot_nt(q, k) + bias for q, k, bias in zip(qs, ks, biases)], vs)


def _softmax_of(a, vs):
    m = [jnp.max(x, axis=-1, keepdims=True) for x in a]
    p = [jnp.exp2(x - mm) for x, mm in zip(a, m)]
    acc = [_dot(x.astype(BF16), v) for x, v in zip(p, vs)]
    return list(zip(m, acc))


def _online_softmax_of(a, v_tile, states):
    m_new = [jnp.maximum(st[0], jnp.max(x, axis=-1, keepdims=True)) for x, st in zip(a, states)]
    alpha = [jnp.exp2(st[0] - mn) for st, mn in zip(states, m_new)]
    p = [jnp.exp2(x - mn) for x, mn in zip(a, m_new)]
    pv = [_dot(x.astype(BF16), v_tile) for x in p]
    return tuple((mn, al * st[1] + o) for mn, al, st, o in zip(m_new, alpha, states, pv))


def _normalised(acc):
    return acc[:, :HEAD_DIM] / acc[:, HEAD_DIM:HEAD_DIM + 1]


def _with_ones(v):
    lane = lax.broadcasted_iota(jnp.int32, v.shape, 1)
    return jnp.concatenate([v, jnp.where(lane == 0, 1.0, 0.0)], axis=1)


def _slope_pieces(slopes):
    hi = slopes.astype(BF16).astype(F32)
    mid = (slopes - hi).astype(BF16).astype(F32)
    lo = (slopes - hi - mid).astype(BF16).astype(F32)
    return jnp.stack([hi, mid, lo], axis=1).reshape(-1)


def _alibi_query_cols(pieces, rows):
    lane = lax.broadcasted_iota(jnp.int32, (rows, HEAD_DIM), 1)
    out = jnp.zeros((rows, HEAD_DIM), F32)
    for n, piece in enumerate(pieces):
        out = jnp.where(lane == n, piece * POS_RADIX, jnp.where(lane == n + 3, piece, out))
    return out


def _alibi_key_cols(pos):
    lane = lax.broadcasted_iota(jnp.int32, pos.shape, 1)
    hi = lax.shift_right_logical(pos, POS_SHIFT).astype(F32)
    lo = jnp.bitwise_and(pos, POS_RADIX - 1).astype(F32)
    return jnp.where(lane < 3, hi, jnp.where(lane < 6, lo, 0.0))


def _dilated_kernel(pieces_ref, q_ref, k_ref, v_ref, qg_ref, kg_ref, lw_ref, o_ref,
                    qn_ref, kn_ref, vb_ref, qr_ref, kr_ref, vr_ref, n32_ref, far_ref, *, seq):
    tile = ATT_TILE
    pair = pl.program_id(1)
    n_tiles = seq // tile
    back = DIL_BACK // tile
    stride = DIL_FAR[1]
    n_res = seq // stride
    far_mask = jnp.where(lax.broadcasted_iota(jnp.int32, (n_res, n_res), 0)
                         >= lax.broadcasted_iota(jnp.int32, (n_res, n_res), 1), 0.0, NEG)

    pos = lax.broadcasted_iota(jnp.int32, (seq, HEAD_DIM), 0)

    def copies(hh, values, nat_ref, res_ref):
        n32_ref[...] = values
        nat_ref[hh] = values.astype(BF16)
        for r in range(stride):
            res_ref[hh, r * n_res:(r + 1) * n_res, :] = n32_ref[pl.ds(r, n_res, stride=stride), :].astype(BF16)

    for hh in range(2):
        lanes = slice(hh * HEAD_DIM, (hh + 1) * HEAD_DIM)
        pieces = [pieces_ref[(2 * pair + hh) * 3 + n] for n in range(3)]
        q_aug = jnp.concatenate([_rms(q_ref[:, lanes].astype(F32), qg_ref[...]) * (ATTN_SCALE * LOG2E),
                                 _alibi_query_cols(pieces, seq)], axis=1)
        k_aug = jnp.concatenate([_rms(k_ref[:, lanes].astype(F32), kg_ref[...]), _alibi_key_cols(pos)], axis=1)
        copies(hh, q_aug, qn_ref, qr_ref)
        copies(hh, k_aug, kn_ref, kr_ref)
        copies(hh, _with_ones(v_ref[:, lanes].astype(F32)), vb_ref, vr_ref)

    def far_step(step, carry):
        res = [FAR_UNROLL * step + n for n in range(FAR_UNROLL)]
        units = [(hh, n) for n in range(FAR_UNROLL) for hh in range(2)]
        rows = [pl.ds(pl.multiple_of(r * n_res, n_res), n_res) for r in res]
        parts = _softmax_blocks([qr_ref[hh, rows[n], :] for hh, n in units],
                                [kr_ref[hh, rows[n], :] for hh, n in units],
                                [far_mask] * len(units),
                                [vr_ref[hh, rows[n], :] for hh, n in units])
        for (hh, n), (m, acc) in zip(units, parts):
            far_ref[hh, pl.ds(res[n], n_res, stride=stride), :] = jnp.concatenate(
                [acc[:, :FAR_MAX_LANE], jnp.broadcast_to(m, (n_res, 2 * HEAD_DIM - FAR_MAX_LANE))], axis=1)
        return carry

    lax.fori_loop(0, stride // FAR_UNROLL, far_step, 0)

    def near_tiles(tiles):
        units = [(hh, t) for t in range(len(tiles)) for hh in range(2)]
        rows = [_tile_rows(i, 1) for i, _, _ in tiles]
        krows = [_tile_rows(first, n_win) for _, first, n_win in tiles]
        bias = [lw_ref[:, (back + 1 - n_win) * tile:] for _, _, n_win in tiles]
        parts = _softmax_blocks([qn_ref[hh, rows[t], :] for hh, t in units],
                                [kn_ref[hh, krows[t], :] for hh, t in units],
                                [bias[t] for hh, t in units],
                                [vb_ref[hh, krows[t], :] for hh, t in units])
        outs = {}
        for (hh, t), (m1, acc1) in zip(units, parts):
            far = far_ref[hh, rows[t], :]
            m2 = far[:, FAR_MAX_LANE:FAR_MAX_LANE + 1]
            m = jnp.maximum(m1, m2)
            outs[hh, t] = _normalised(jnp.exp2(m1 - m) * acc1 + jnp.exp2(m2 - m) * far)
        for t in range(len(tiles)):
            o_ref[rows[t], :] = jnp.concatenate([outs[0, t], outs[1, t]], axis=1).astype(BF16)

    head_tiles = min(back, n_tiles)
    near_tiles([(i, 0, i + 1) for i in range(head_tiles)])

    def near_body(step, carry):
        first = head_tiles + NEAR_UNROLL * step
        near_tiles([(first + n, first + n - back, back + 1) for n in range(NEAR_UNROLL)])
        return carry

    lax.fori_loop(0, (n_tiles - head_tiles) // NEAR_UNROLL, near_body, 0)


def _dilated(proj, q_gain, k_gain, *, batch, seq, col0):
    pairs = 4
    heads = 2 * pairs
    slopes = (2.0 ** (-8.0 * jnp.arange(1, heads + 1, dtype=F32) / heads)).astype(F32)
    assert seq <= DIL_FAR[0] and seq % (DIL_FAR[1] * 8) == 0 and seq % ATT_TILE == 0
    assert seq <= POS_RADIX * 256 and DIL_FAR[1] % FAR_UNROLL == 0
    assert (seq // ATT_TILE - min(DIL_BACK // ATT_TILE, seq // ATT_TILE)) % NEAR_UNROLL == 0
    win = DIL_BACK + ATT_TILE

    def col(off):
        return pl.BlockSpec((seq, 2 * HEAD_DIM), lambda b, p: (b, col0 + off * pairs + p))

    head_copy = pltpu.VMEM((2, seq, 2 * HEAD_DIM), BF16)
    return pl.pallas_call(
        functools.partial(_dilated_kernel, seq=seq),
        grid=(batch, pairs),
        in_specs=[pl.BlockSpec(memory_space=pltpu.SMEM),
                  col(0), col(1), col(2),
                  pl.BlockSpec((1, HEAD_DIM), lambda b, p: (0, 0)),
                  pl.BlockSpec((1, HEAD_DIM), lambda b, p: (0, 0)),
                  pl.BlockSpec((ATT_TILE, win), lambda b, p: (0, 0))],
        out_specs=pl.BlockSpec((seq, 2 * HEAD_DIM), lambda b, p: (b, p)),
        out_shape=jax.ShapeDtypeStruct((batch * seq, heads * HEAD_DIM), BF16),
        scratch_shapes=[head_copy] * 6 + [pltpu.VMEM((seq, 2 * HEAD_DIM), F32),
                                          pltpu.VMEM((2, seq, 2 * HEAD_DIM), F32)],
        compiler_params=_cparams("parallel", "parallel"),
        name="dilated",
    )(_slope_pieces(slopes * LOG2E), proj, proj, proj, q_gain.reshape(1, -1), k_gain.reshape(1, -1),
      jnp.asarray(_dilated_near_log_weights()))


def _nsa_overlap_t(n_blocks, n_cmp_rows):
    blk = np.arange(n_blocks)[:, None]
    start = np.arange(n_cmp_rows)[None, :] * CMP_STRIDE
    return ((start < (blk + 1) * SLC_BLOCK) & (start + CMP_LEN > blk * SLC_BLOCK)).astype(np.float32)


def _nsa_kernel(pieces_ref, x_ref, qg_ref, kg_ref, pe_ref, w1_ref, w2_ref, ovl_ref, o_ref,
                qn_ref, ks_ref, vs_ref, kw_ref, vw_ref, kc_ref, vc_ref, rc_slc_ref, rc_win_ref, *, seq):
    tile = NSA_TILE
    n_tiles = seq // tile
    n_cmp = seq // CMP_STRIDE
    n_blk = seq // SLC_BLOCK
    grp = pl.program_id(1)
    heads = range(C_GROUP)

    prep_rows = min(NSA_PREP_ROWS, seq)
    q_cols = [_alibi_query_cols([pieces_ref[(grp * C_GROUP + j) * 3 + n] for n in range(3)], prep_rows)
              for j in heads]
    row_pos = lax.broadcasted_iota(jnp.int32, (prep_rows, HEAD_DIM), 0)

    def prep(i, carry):
        rows = pl.ds(pl.multiple_of(i * prep_rows, prep_rows), prep_rows)
        k_cols = _alibi_key_cols(i * prep_rows + row_pos)
        x = x_ref[rows, :NSA_GATE_COL].astype(F32)
        for j in heads:
            qj = _rms(x[:, j * HEAD_DIM:(j + 1) * HEAD_DIM], qg_ref[...]) * (ATTN_SCALE * LOG2E)
            qj = jnp.concatenate([qj, q_cols[j]], axis=1).astype(BF16)
            for t in range(prep_rows // tile):
                at = ((i * (prep_rows // tile) + t) * C_GROUP + j) * tile
                qn_ref[pl.ds(pl.multiple_of(at, tile), tile), :] = qj[t * tile:(t + 1) * tile]
        kc_ref[rows, :] = x[:, 256:320]
        vc_ref[rows, :] = x[:, 320:384]
        ks_ref[rows, :] = jnp.concatenate([_rms(x[:, 384:448], kg_ref[1:2]), k_cols], axis=1).astype(BF16)
        vs_ref[rows, :] = _with_ones(x[:, 448:512]).astype(BF16)
        kw_ref[rows, :] = jnp.concatenate([_rms(x[:, 512:576], kg_ref[2:3]), k_cols], axis=1).astype(BF16)
        vw_ref[rows, :] = _with_ones(x[:, 576:640]).astype(BF16)
        return carry

    lax.fori_loop(0, seq // prep_rows, prep, 0)
    kc_ref[seq:, :] = jnp.zeros((CMP_LEN, HEAD_DIM), F32)
    vc_ref[seq:, :] = jnp.zeros((CMP_LEN, HEAD_DIM), F32)

    def compress(src_ref, which):
        hid = jnp.zeros((n_cmp, CMP_HIDDEN), F32)
        for pos in range(CMP_LEN):
            tok = src_ref[pl.ds(pos, n_cmp, stride=CMP_STRIDE), :] + pe_ref[which, pos:pos + 1, :]
            hid = hid + _dot(tok.astype(BF16), w1_ref[which, pos * HEAD_DIM:(pos + 1) * HEAD_DIM, :])
        return _dot(_silu(hid).astype(BF16), w2_ref[which])

    cmp_pos = lax.broadcasted_iota(jnp.int32, (n_cmp, HEAD_DIM), 0) * CMP_STRIDE + (CMP_LEN - 1)
    k_cmp = jnp.concatenate([_rms(compress(kc_ref, 0), kg_ref[0:1]), _alibi_key_cols(cmp_pos)],
                            axis=1).astype(BF16)
    v_cmp = compress(vc_ref, 1).astype(BF16)
    ovl = ovl_ref[...].astype(BF16)

    rc_slc_ref[...] = (lax.broadcasted_iota(jnp.int32, (tile, SLC_CHUNK), 0)
                       - lax.broadcasted_iota(jnp.int32, (tile, SLC_CHUNK), 1)).astype(F32)
    rc_win_ref[...] = (lax.broadcasted_iota(jnp.int32, (tile, WIN + tile), 0)
                       - lax.broadcasted_iota(jnp.int32, (tile, WIN + tile), 1)).astype(F32)
    blk_row = lax.broadcasted_iota(jnp.int32, (tile, SLC_CHUNK), 0)
    blk_of_key = lax.broadcasted_iota(jnp.int32, (tile, SLC_CHUNK), 1) // SLC_BLOCK
    cmp_end = (lax.broadcasted_iota(jnp.int32, (tile, n_cmp), 1) * CMP_STRIDE + (CMP_LEN - 1))
    blk_i = lax.broadcasted_iota(jnp.int32, (n_blk, tile), 0)
    tok_lane = lax.broadcasted_iota(jnp.int32, (n_blk, tile), 1)

    def head_rows(s):
        return [s[j * tile:(j + 1) * tile] for j in heads]

    def q_tile(i, carry):
        t0 = i * tile
        rows = pl.ds(pl.multiple_of(t0, tile), tile)
        q4 = qn_ref[pl.ds(pl.multiple_of(i * C_GROUP * tile, tile), C_GROUP * tile), :]

        vis_c = (t0 + lax.broadcasted_iota(jnp.int32, (tile, n_cmp), 0)) >= cmp_end
        a_c = [jnp.where(vis_c, x, NEG) for x in head_rows(_dot_nt(q4, k_cmp))]
        m_c = [jnp.max(x, axis=-1, keepdims=True) for x in a_c]
        e_c = [jnp.where(vis_c, jnp.exp2(x - m), 0.0) for x, m in zip(a_c, m_c)]
        l_c = [jnp.sum(x, axis=-1, keepdims=True) for x in e_c]
        p_c = [x / jnp.where(l > 0, l, 1.0) for x, l in zip(e_c, l_c)]
        o_cmp = [_dot(x.astype(BF16), v_cmp) for x in p_c]
        hi, mid, lo = _split3(p_c[0] + p_c[1] + p_c[2] + p_c[3])
        imp = _dot_nt(ovl, hi) + _dot_nt(ovl, mid) + _dot_nt(ovl, lo)

        cur = lax.shift_right_logical(t0 + tok_lane, 6)
        forced = (blk_i == 0) | (blk_i == cur) | (blk_i == cur - 1)
        score = jnp.where(blk_i > cur, -1.0, jnp.where(forced, 1e30, imp))
        rank = jnp.zeros((n_blk, tile), F32)
        for other in range(n_blk):
            s_o = score[other:other + 1, :]
            rank = rank + jnp.where(s_o > score, 1.0,
                                    jnp.where(s_o == score, jnp.where(blk_i > other, 1.0, 0.0), 0.0))
        sel = jnp.where(rank < SLC_TOPN, jnp.where(score >= 0, 1.0, 0.0), 0.0)
        sel_t = jnp.concatenate([sel, jnp.zeros((tile - n_blk, tile), F32)], axis=0).T.astype(BF16)

        def slc_step(c, st):
            krows = pl.ds(pl.multiple_of(c * SLC_CHUNK, SLC_CHUNK), SLC_CHUNK)
            expand = jnp.where(blk_row == c * (SLC_CHUNK // SLC_BLOCK) + blk_of_key, 1.0, 0.0).astype(BF16)
            picked = _dot(sel_t, expand)
            dist = rc_slc_ref[...] + jnp.asarray(t0 - c * SLC_CHUNK, F32)
            mask = jnp.where(picked > 0.5, jnp.where(dist >= 0, 0.0, NEG), NEG)
            a = [x + mask for x in head_rows(_dot_nt(q4, ks_ref[krows, :]))]
            return _online_softmax_of(a, vs_ref[krows, :], st)

        init = tuple((jnp.full((tile, 1), NEG, F32), jnp.zeros((tile, 2 * HEAD_DIM), F32)) for _ in heads)
        slc = lax.fori_loop(0, (t0 + tile - 1) // SLC_CHUNK + 1, slc_step, init)

        first = jnp.maximum(i - WIN // tile, 0)
        krows = pl.ds(pl.multiple_of(first * tile, tile), WIN + tile)
        dist = rc_win_ref[...] + jnp.asarray((i - first) * tile, F32)
        mask = jnp.where(dist >= 0, jnp.where(dist < WIN, 0.0, NEG), NEG)
        win = _softmax_of([x + mask for x in head_rows(_dot_nt(q4, kw_ref[krows, :]))],
                          [vw_ref[krows, :]] * C_GROUP)

        gate = jax.nn.sigmoid(x_ref[rows, NSA_GATE_COL:NSA_GATE_COL + 128].astype(F32))
        outs = []
        for j in heads:
            outs.append(gate[:, j:j + 1] * o_cmp[j]
                        + gate[:, C_GROUP + j:C_GROUP + j + 1] * _normalised(slc[j][1])
                        + gate[:, 2 * C_GROUP + j:2 * C_GROUP + j + 1] * _normalised(win[j][1]))
        o_ref[rows, :] = jnp.concatenate(outs, axis=1).astype(BF16)
        return carry

    lax.fori_loop(0, n_tiles, q_tile, 0)


def _nsa(proj, q_gain, k_gain, cmp_pe, cmp_w1, cmp_w2, *, batch, seq):
    heads = C_KV_HEADS * C_GROUP
    slopes = (2.0 ** (-8.0 * jnp.arange(1, heads + 1, dtype=F32) / heads)).astype(F32)
    n_cmp = seq // CMP_STRIDE
    n_blk = seq // SLC_BLOCK
    assert seq % SLC_CHUNK == 0 and seq >= WIN + NSA_TILE and n_blk <= NSA_TILE and n_cmp % 128 == 0
    assert SLC_CHUNK % NSA_TILE == 0 and WIN % NSA_TILE == 0
    full = lambda *shape: pl.BlockSpec(shape, lambda b, g: (0,) * len(shape))
    head_copy = pltpu.VMEM((seq, 2 * HEAD_DIM), BF16)
    return pl.pallas_call(
        functools.partial(_nsa_kernel, seq=seq),
        grid=(batch, C_KV_HEADS),
        in_specs=[pl.BlockSpec(memory_space=pltpu.SMEM),
                  pl.BlockSpec((seq, NSA_GROUP_COLS), lambda b, g: (b, g)),
                  full(1, HEAD_DIM), full(3, HEAD_DIM), full(2, CMP_LEN, HEAD_DIM),
                  full(2, CMP_LEN * HEAD_DIM, CMP_HIDDEN), full(2, CMP_HIDDEN, HEAD_DIM),
                  full(n_blk, n_cmp)],
        out_specs=pl.BlockSpec((seq, C_GROUP * HEAD_DIM), lambda b, g: (b, g)),
        out_shape=jax.ShapeDtypeStruct((batch * seq, heads * HEAD_DIM), BF16),
        scratch_shapes=[pltpu.VMEM((seq * C_GROUP, 2 * HEAD_DIM), BF16)] + [head_copy] * 4 + [
            pltpu.VMEM((seq + CMP_LEN, HEAD_DIM), F32),
            pltpu.VMEM((seq + CMP_LEN, HEAD_DIM), F32),
            pltpu.VMEM((NSA_TILE, SLC_CHUNK), F32),
            pltpu.VMEM((NSA_TILE, WIN + NSA_TILE), F32)],
        compiler_params=_cparams("parallel", "parallel"),
        name="nsa",
    )(_slope_pieces(slopes * LOG2E), proj, q_gain.reshape(1, -1), k_gain, cmp_pe,
      cmp_w1.astype(BF16), cmp_w2.astype(BF16), jnp.asarray(_nsa_overlap_t(n_blk, n_cmp)))


def _nsa_in_weight(w_in, d_model):
    kv_w = C_KV_HEADS * HEAD_DIM
    cols = []
    for g in range(C_KV_HEADS):
        idx = list(range(g * C_GROUP * HEAD_DIM, (g + 1) * C_GROUP * HEAD_DIM))
        for part in range(6):
            base = d_model + part * kv_w + g * HEAD_DIM
            idx += list(range(base, base + HEAD_DIM))
        gate0 = d_model + 6 * kv_w
        idx += [gate0 + n * C_KV_HEADS * C_GROUP + g * C_GROUP + j for n in range(3) for j in range(C_GROUP)]
        cols.append(jnp.pad(w_in[:, np.asarray(idx)], ((0, 0), (0, NSA_GROUP_COLS - len(idx)))))
    return jnp.concatenate(cols, axis=1)


def kernel(x, attn_norm, ffn_norm, hy_w_in, hy_lb_logits, hy_o_gain, hy_q_gain, hy_k_gain, hy_w_out, nsa_w_in, nsa_q_gain, nsa_k_gain, nsa_cmp_pe, nsa_cmp_w1, nsa_cmp_w2, nsa_w_out, ffn_w_gate, ffn_w_up, ffn_w_down):
    batch, seq, d_model = x.shape
    depth = attn_norm.shape[0]
    a_width = hy_o_gain.shape[1]
    dff = ffn_w_gate.shape[2]
    h = x.reshape(batch * seq, d_model)
    for layer in range(depth):
        i = layer // 2
        if layer % 2 == 0:
            proj = _norm_matmul(h, attn_norm[layer], hy_w_in[i].astype(BF16), tm=1024,
                                tn=hy_w_in.shape[2] // 2)
            a_out = _hgrn2(proj, hy_lb_logits, hy_o_gain[i], layer_idx=i, batch=batch, seq=seq,
                           heads_per_step=4, chunks_per_step=4)
            b_out = _dilated(proj, hy_q_gain[i], hy_k_gain[i], batch=batch, seq=seq,
                             col0=4 * a_width // (2 * HEAD_DIM))
            w_out = hy_w_out[i].astype(BF16)
            mixed, w_mix = [a_out, b_out], [w_out[:a_width], w_out[a_width:]]
        else:
            proj = _norm_matmul(h, attn_norm[layer], _nsa_in_weight(nsa_w_in[i], d_model).astype(BF16),
                                tm=1024, tn=2 * NSA_GROUP_COLS)
            o = _nsa(proj, nsa_q_gain[i], nsa_k_gain[i], nsa_cmp_pe[i], nsa_cmp_w1[i], nsa_cmp_w2[i],
                     batch=batch, seq=seq)
            mixed, w_mix = [o], [nsa_w_out[i].astype(BF16)]
        h = _mix_ffn(h, mixed, w_mix, ffn_norm[layer], ffn_w_gate[layer].astype(BF16),
                     ffn_w_up[layer].astype(BF16), ffn_w_down[layer].astype(BF16), tm=1024, tf=dff // 2)
    return h.reshape(batch, seq, d_model)
```

```python
import functools

import numpy as np
import jax
import jax.numpy as jnp
from jax import lax
from jax.experimental import pallas as pl
from jax.experimental.pallas import tpu as pltpu

F32 = jnp.float32
BF16 = jnp.bfloat16

EPS = 1e-6
HEAD_DIM = 64
ATTN_SCALE = HEAD_DIM ** -0.5
NEG = -1e30

A_EXPAND = 128
A_CHUNK = 32
A_SUB = 8

DILATED_PATTERNS = ((128, 1), (512, 4), (2048, 16))
ATT_TILE = 128

C_KV_HEADS = 4
C_GROUP = 4
CMP_LEN = 32
CMP_STRIDE = 16
CMP_HIDDEN = 2 * HEAD_DIM
SLC_BLOCK = 64
SLC_TOPN = 8
SLC_CHUNK = 512
WIN = 512
NSA_GROUP_COLS = 768
NSA_GATE_COL = 640
NSA_TILE = 256
NSA_PREP_ROWS = 2048

V7X_VMEM_LIMIT = 56 * 1024 * 1024


def _cparams(*sem):
    return pltpu.CompilerParams(dimension_semantics=sem, vmem_limit_bytes=V7X_VMEM_LIMIT)


def _rms(x, gain):
    return x * lax.rsqrt(jnp.mean(x * x, axis=-1, keepdims=True) + EPS) * gain


def _silu(x):
    return x * jax.nn.sigmoid(x)


def _split3(x):
    hi = x.astype(BF16)
    r1 = x - hi.astype(F32)
    mid = r1.astype(BF16)
    lo = (r1 - mid.astype(F32)).astype(BF16)
    return hi, mid, lo


def _dot(a, b):
    return jnp.dot(a, b, preferred_element_type=F32)


def _dot_nt(a, b):
    return lax.dot_general(a, b, (((1,), (1,)), ((), ())), preferred_element_type=F32)


def _dot_tn(a, b):
    return lax.dot_general(a, b, (((0,), (0,)), ((), ())), preferred_element_type=F32)


def _norm_matmul_kernel(x_ref, g_ref, w_ref, o_ref, h_ref):
    @pl.when(pl.program_id(1) == 0)
    def _():
        h_ref[...] = _rms(x_ref[...], g_ref[...]).astype(BF16)

    o_ref[...] = _dot(h_ref[...], w_ref[...]).astype(o_ref.dtype)


def _norm_matmul(x, gain, w, *, tm, tn):
    t, d = x.shape
    n = w.shape[1]
    return pl.pallas_call(
        _norm_matmul_kernel,
        grid=(t // tm, n // tn),
        in_specs=[pl.BlockSpec((tm, d), lambda i, j: (i, 0)),
                  pl.BlockSpec((1, d), lambda i, j: (0, 0)),
                  pl.BlockSpec((d, tn), lambda i, j: (0, j))],
        out_specs=pl.BlockSpec((tm, tn), lambda i, j: (i, j)),
        out_shape=jax.ShapeDtypeStruct((t, n), BF16),
        scratch_shapes=[pltpu.VMEM((tm, d), BF16)],
        compiler_params=_cparams("parallel", "arbitrary"),
        name="norm_matmul",
    )(x, gain.reshape(1, d), w)


def _lane_tile_halves(width):
    mid = (width // 256) * 128 if width >= 256 else width
    return [(0, mid), (mid, width)] if mid < width else [(0, width)]


def _mix_ffn_kernel(*refs, n_in):
    x_ref = refs[0]
    part_refs, w_refs = refs[1:1 + n_in], refs[1 + n_in:1 + 2 * n_in]
    g_ref, wg_ref, wu_ref, wd_ref, o_ref, h_ref = refs[1 + 2 * n_in:]

    @pl.when(pl.program_id(1) == 0)
    def _():
        x = x_ref[...]
        for a_ref, w_ref in zip(part_refs, w_refs):
            x = x + _dot(a_ref[...], w_ref[...])
        h_ref[...] = _rms(x, g_ref[...]).astype(BF16)
        o_ref[...] = x

    h = h_ref[...]
    tf = wg_ref.shape[1]
    for lo_, hi_ in _lane_tile_halves(tf):
        act = _silu(_dot(h, wg_ref[:, lo_:hi_])) * _dot(h, wu_ref[:, lo_:hi_])
        o_ref[...] += _dot(act.astype(BF16), wd_ref[lo_:hi_, :])


def _mix_ffn(x, parts, weights, gain, w_gate, w_up, w_down, *, tm, tf):
    t, d = x.shape
    dff = w_gate.shape[1]
    n_in = len(parts)
    in_specs = [pl.BlockSpec((tm, d), lambda i, f: (i, 0))]
    in_specs += [pl.BlockSpec((tm, p.shape[1]), lambda i, f: (i, 0)) for p in parts]
    in_specs += [pl.BlockSpec(w.shape, lambda i, f: (0, 0)) for w in weights]
    in_specs += [pl.BlockSpec((1, d), lambda i, f: (0, 0)),
                 pl.BlockSpec((d, tf), lambda i, f: (0, f)),
                 pl.BlockSpec((d, tf), lambda i, f: (0, f)),
                 pl.BlockSpec((tf, d), lambda i, f: (f, 0))]
    return pl.pallas_call(
        functools.partial(_mix_ffn_kernel, n_in=n_in),
        grid=(t // tm, dff // tf),
        in_specs=in_specs,
        out_specs=pl.BlockSpec((tm, d), lambda i, f: (i, 0)),
        out_shape=jax.ShapeDtypeStruct((t, d), F32),
        scratch_shapes=[pltpu.VMEM((tm, d), BF16)],
        compiler_params=_cparams("parallel", "arbitrary"),
        name="mix_ffn",
    )(x, *parts, *weights, gain.reshape(1, d), w_gate, w_up, w_down)


def _hgrn2_kernel(q_ref, f_ref, v_ref, g_ref, lbl_ref, og_ref, o_ref, state_ref, *,
                  layer_idx, seq, heads, unroll):
    c_len, hd = A_CHUNK, A_EXPAND
    logits = lbl_ref[...]
    e = jnp.exp(logits - jnp.max(logits, axis=0, keepdims=True))
    prob = e / jnp.sum(e, axis=0, keepdims=True)
    lb_all = jnp.sum(prob[:layer_idx + 1], axis=0, keepdims=True) - prob[0:1]
    og_all = og_ref[...]

    sub, n_sub = A_SUB, c_len // A_SUB
    row = lax.broadcasted_iota(jnp.int32, (c_len, hd), 0)
    row_sub = lax.broadcasted_iota(jnp.int32, (sub, hd), 0)
    lane_sub = lax.broadcasted_iota(jnp.int32, (sub, c_len), 1)
    tril = (lax.broadcasted_iota(jnp.int32, (c_len, c_len), 0)
            >= lax.broadcasted_iota(jnp.int32, (c_len, c_len), 1)).astype(BF16)
    ones = jnp.ones((hd, hd), BF16)

    state_ref[...] = jnp.zeros_like(state_ref)

    def step(c, carry):
        chains = [(h, u) for h in range(heads) for u in range(unroll)]
        rows = {u: pl.ds(pl.multiple_of((c * unroll + u) * c_len, c_len), c_len) for u in range(unroll)}
        cols = {h: slice(h * hd, (h + 1) * hd) for h in range(heads)}

        qq, kk, v, b = {}, {}, {}, {}
        for ch in chains:
            h, u = ch
            lb = lb_all[:, cols[h]]
            f = lb + (1.0 - lb) * jax.nn.sigmoid(f_ref[rows[u], cols[h]].astype(F32))
            kk[ch] = 1.0 - f
            qq[ch] = _silu(q_ref[rows[u], cols[h]].astype(F32))
            v[ch] = v_ref[rows[u], cols[h]].astype(BF16)
            hi, mid, lo = _split3(jnp.log(f))
            b[ch] = _dot(tril, hi) + _dot(tril, mid) + _dot(tril, lo)

        r = {}
        for ch in chains:
            pieces = []
            for i in range(n_sub):
                lo_ = i * sub
                for s in range(sub):
                    d = b[ch][lo_:lo_ + sub] - b[ch][lo_ + s:lo_ + s + 1]
                    e_s = jnp.exp(jnp.where(row_sub >= s, d, NEG))
                    pieces.append(qq[ch][lo_:lo_ + sub] * e_s * kk[ch][lo_ + s:lo_ + s + 1])
            r[ch] = _dot(jnp.concatenate(pieces, axis=0).astype(BF16), ones)[:, :c_len]

        o = {}
        for ch in chains:
            blocks = []
            for i in range(n_sub):
                lo_ = i * sub
                sc = jnp.zeros((sub, c_len), F32)
                for s in range(sub):
                    at = (lo_ + s) * sub
                    sc = jnp.where(lane_sub == lo_ + s, r[ch][at:at + sub], sc)
                if i > 0:
                    b_ref_row = b[ch][lo_ - 1:lo_]
                    q_dec = (qq[ch][lo_:lo_ + sub] * jnp.exp(b[ch][lo_:lo_ + sub] - b_ref_row)).astype(BF16)
                    k_grow = (kk[ch] * jnp.exp(jnp.where(row < lo_, b_ref_row - b[ch], NEG))).astype(BF16)
                    sc = sc + _dot_nt(q_dec, k_grow)
                blocks.append(sc)
            o[ch] = _dot(jnp.concatenate(blocks, axis=0).astype(BF16), v[ch])

        upd, q_all = {}, {}
        for ch in chains:
            b_last = b[ch][c_len - 1:c_len]
            k_dec = (kk[ch] * jnp.exp(b_last - b[ch])).astype(BF16)
            upd[ch] = (jnp.exp(b_last), _dot_tn(v[ch], k_dec))
            q_all[ch] = (qq[ch] * jnp.exp(b[ch])).astype(BF16)

        for h in range(heads):
            st = state_ref[h]
            for u in range(unroll):
                ch = (h, u)
                o_c = o[ch] + _dot_nt(q_all[ch], st.astype(BF16))
                st = st * upd[ch][0] + upd[ch][1]
                o_c = o_c * lax.rsqrt(jnp.mean(o_c * o_c, axis=-1, keepdims=True) + EPS)
                gate = _silu(g_ref[rows[u], cols[h]].astype(F32))
                o_ref[rows[u], cols[h]] = (o_c * og_all[:, cols[h]] * gate).astype(BF16)
            state_ref[h] = st
        return carry

    lax.fori_loop(0, seq // (c_len * unroll), step, 0)


def _hgrn2(proj, lb_logits, o_gain, *, layer_idx, batch, seq, heads_per_step, chunks_per_step):
    heads = o_gain.shape[0] // A_EXPAND
    groups = heads // heads_per_step
    width = heads_per_step * A_EXPAND
    n_layers = lb_logits.shape[0]

    def col(off):
        return pl.BlockSpec((seq, width), lambda b, h: (b, h + off * groups))

    return pl.pallas_call(
        functools.partial(_hgrn2_kernel, layer_idx=layer_idx, seq=seq, heads=heads_per_step,
                          unroll=chunks_per_step),
        grid=(batch, groups),
        in_specs=[col(0), col(1), col(2), col(3),
                  pl.BlockSpec((n_layers, width), lambda b, h: (0, h)),
                  pl.BlockSpec((1, width), lambda b, h: (0, h))],
        out_specs=pl.BlockSpec((seq, width), lambda b, h: (b, h)),
        out_shape=jax.ShapeDtypeStruct((batch * seq, heads * A_EXPAND), BF16),
        scratch_shapes=[pltpu.VMEM((heads_per_step, A_EXPAND, A_EXPAND), F32)],
        compiler_params=_cparams("parallel", "parallel"),
        name="hgrn2",
    )(proj, proj, proj, proj, lb_logits, o_gain.reshape(1, -1))


DIL_NEAR = DILATED_PATTERNS[:2]
DIL_FAR = DILATED_PATTERNS[2]
DIL_BACK = max(w for w, _ in DIL_NEAR)


def _dilated_near_log_weights():
    r = np.arange(ATT_TILE)[:, None]
    c = np.arange(DIL_BACK + ATT_TILE)[None, :]
    d = r + DIL_BACK - c
    mult = np.zeros_like(d)
    for window, dilation in DIL_NEAR:
        mult += (d >= 0) & (d % dilation == 0) & (d <= window)
    return np.where(mult > 0, np.log2(np.maximum(mult, 1)), NEG).astype(np.float32)


FAR_MAX_LANE = 96
LOG2E = 1.4426950408889634
FAR_UNROLL = 8
NEAR_UNROLL = 6
POS_SHIFT = 7
POS_RADIX = 1 << POS_SHIFT


def _tile_rows(first_tile, n_tiles):
    start = first_tile * ATT_TILE
    if not isinstance(first_tile, int):
        start = pl.multiple_of(start, ATT_TILE)
    return pl.ds(start, n_tiles * ATT_TILE)


def _softmax_blocks(qs, ks, biases, vs):
    return _softmax_of([_dot_nt(q, k) + bias for q, k, bias in zip(qs, ks, biases)], vs)


def _softmax_of(a, vs):
    m = [jnp.max(x, axis=-1, keepdims=True) for x in a]
    p = [jnp.exp2(x - mm) for x, mm in zip(a, m)]
    acc = [_dot(x.astype(BF16), v) for x, v in zip(p, vs)]
    return list(zip(m, acc))


def _online_softmax_of(a, v_tile, states):
    m_new = [jnp.maximum(st[0], jnp.max(x, axis=-1, keepdims=True)) for x, st in zip(a, states)]
    alpha = [jnp.exp2(st[0] - mn) for st, mn in zip(states, m_new)]
    p = [jnp.exp2(x - mn) for x, mn in zip(a, m_new)]
    pv = [_dot(x.astype(BF16), v_tile) for x in p]
    return tuple((mn, al * st[1] + o) for mn, al, st, o in zip(m_new, alpha, states, pv))


def _normalised(acc):
    return acc[:, :HEAD_DIM] / acc[:, HEAD_DIM:HEAD_DIM + 1]


def _with_ones(v):
    lane = lax.broadcasted_iota(jnp.int32, v.shape, 1)
    return jnp.concatenate([v, jnp.where(lane == 0, 1.0, 0.0)], axis=1)


def _slope_pieces(slopes):
    hi = slopes.astype(BF16).astype(F32)
    mid = (slopes - hi).astype(BF16).astype(F32)
    lo = (slopes - hi - mid).astype(BF16).astype(F32)
    return jnp.stack([hi, mid, lo], axis=1).reshape(-1)


def _alibi_query_cols(pieces, rows):
    lane = lax.broadcasted_iota(jnp.int32, (rows, HEAD_DIM), 1)
    out = jnp.zeros((rows, HEAD_DIM), F32)
    for n, piece in enumerate(pieces):
        out = jnp.where(lane == n, piece * POS_RADIX, jnp.where(lane == n + 3, piece, out))
    return out


def _alibi_key_cols(pos):
    lane = lax.broadcasted_iota(jnp.int32, pos.shape, 1)
    hi = lax.shift_right_logical(pos, POS_SHIFT).astype(F32)
    lo = jnp.bitwise_and(pos, POS_RADIX - 1).astype(F32)
    return jnp.where(lane < 3, hi, jnp.where(lane < 6, lo, 0.0))


def _dilated_kernel(pieces_ref, q_ref, k_ref, v_ref, qg_ref, kg_ref, lw_ref, o_ref,
                    qn_ref, kn_ref, vb_ref, qr_ref, kr_ref, vr_ref, n32_ref, far_ref, *, seq):
    tile = ATT_TILE
    pair = pl.program_id(1)
    n_tiles = seq // tile
    back = DIL_BACK // tile
    stride = DIL_FAR[1]
    n_res = seq // stride
    far_mask = jnp.where(lax.broadcasted_iota(jnp.int32, (n_res, n_res), 0)
                         >= lax.broadcasted_iota(jnp.int32, (n_res, n_res), 1), 0.0, NEG)

    pos = lax.broadcasted_iota(jnp.int32, (seq, HEAD_DIM), 0)

    def copies(hh, values, nat_ref, res_ref):
        n32_ref[...] = values
        nat_ref[hh] = values.astype(BF16)
        for r in range(stride):
            res_ref[hh, r * n_res:(r + 1) * n_res, :] = n32_ref[pl.ds(r, n_res, stride=stride), :].astype(BF16)

    for hh in range(2):
        lanes = slice(hh * HEAD_DIM, (hh + 1) * HEAD_DIM)
        pieces = [pieces_ref[(2 * pair + hh) * 3 + n] for n in range(3)]
        q_aug = jnp.concatenate([_rms(q_ref[:, lanes].astype(F32), qg_ref[...]) * (ATTN_SCALE * LOG2E),
                                 _alibi_query_cols(pieces, seq)], axis=1)
        k_aug = jnp.concatenate([_rms(k_ref[:, lanes].astype(F32), kg_ref[...]), _alibi_key_cols(pos)], axis=1)
        copies(hh, q_aug, qn_ref, qr_ref)
        copies(hh, k_aug, kn_ref, kr_ref)
        copies(hh, _with_ones(v_ref[:, lanes].astype(F32)), vb_ref, vr_ref)

    def far_step(step, carry):
        res = [FAR_UNROLL * step + n for n in range(FAR_UNROLL)]
        units = [(hh, n) for n in range(FAR_UNROLL) for hh in range(2)]
        rows = [pl.ds(pl.multiple_of(r * n_res, n_res), n_res) for r in res]
        parts = _softmax_blocks([qr_ref[hh, rows[n], :] for hh, n in units],
                                [kr_ref[hh, rows[n], :] for hh, n in units],
                                [far_mask] * len(units),
                                [vr_ref[hh, rows[n], :] for hh, n in units])
        for (hh, n), (m, acc) in zip(units, parts):
            far_ref[hh, pl.ds(res[n], n_res, stride=stride), :] = jnp.concatenate(
                [acc[:, :FAR_MAX_LANE], jnp.broadcast_to(m, (n_res, 2 * HEAD_DIM - FAR_MAX_LANE))], axis=1)
        return carry

    lax.fori_loop(0, stride // FAR_UNROLL, far_step, 0)

    def near_tiles(tiles):
        units = [(hh, t) for t in range(len(tiles)) for hh in range(2)]
        rows = [_tile_rows(i, 1) for i, _, _ in tiles]
        krows = [_tile_rows(first, n_win) for _, first, n_win in tiles]
        bias = [lw_ref[:, (back + 1 - n_win) * tile:] for _, _, n_win in tiles]
        parts = _softmax_blocks([qn_ref[hh, rows[t], :] for hh, t in units],
                                [kn_ref[hh, krows[t], :] for hh, t in units],
                                [bias[t] for hh, t in units],
                                [vb_ref[hh, krows[t], :] for hh, t in units])
        outs = {}
        for (hh, t), (m1, acc1) in zip(units, parts):
            far = far_ref[hh, rows[t], :]
            m2 = far[:, FAR_MAX_LANE:FAR_MAX_LANE + 1]
            m = jnp.maximum(m1, m2)
            outs[hh, t] = _normalised(jnp.exp2(m1 - m) * acc1 + jnp.exp2(m2 - m) * far)
        for t in range(len(tiles)):
            o_ref[rows[t], :] = jnp.concatenate([outs[0, t], outs[1, t]], axis=1).astype(BF16)

    head_tiles = min(back, n_tiles)
    near_tiles([(i, 0, i + 1) for i in range(head_tiles)])

    def near_body(step, carry):
        first = head_tiles + NEAR_UNROLL * step
        near_tiles([(first + n, first + n - back, back + 1) for n in range(NEAR_UNROLL)])
        return carry

    lax.fori_loop(0, (n_tiles - head_tiles) // NEAR_UNROLL, near_body, 0)


def _dilated(proj, q_gain, k_gain, *, batch, seq, col0):
    pairs = 4
    heads = 2 * pairs
    slopes = (2.0 ** (-8.0 * jnp.arange(1, heads + 1, dtype=F32) / heads)).astype(F32)
    assert seq <= DIL_FAR[0] and seq % (DIL_FAR[1] * 8) == 0 and seq % ATT_TILE == 0
    assert seq <= POS_RADIX * 256 and DIL_FAR[1] % FAR_UNROLL == 0
    assert (seq // ATT_TILE - min(DIL_BACK // ATT_TILE, seq // ATT_TILE)) % NEAR_UNROLL == 0
    win = DIL_BACK + ATT_TILE

    def col(off):
        return pl.BlockSpec((seq, 2 * HEAD_DIM), lambda b, p: (b, col0 + off * pairs + p))

    head_copy = pltpu.VMEM((2, seq, 2 * HEAD_DIM), BF16)
    return pl.pallas_call(
        functools.partial(_dilated_kernel, seq=seq),
        grid=(batch, pairs),
        in_specs=[pl.BlockSpec(memory_space=pltpu.SMEM),
                  col(0), col(1), col(2),
                  pl.BlockSpec((1, HEAD_DIM), lambda b, p: (0, 0)),
                  pl.BlockSpec((1, HEAD_DIM), lambda b, p: (0, 0)),
                  pl.BlockSpec((ATT_TILE, win), lambda b, p: (0, 0))],
        out_specs=pl.BlockSpec((seq, 2 * HEAD_DIM), lambda b, p: (b, p)),
        out_shape=jax.ShapeDtypeStruct((batch * seq, heads * HEAD_DIM), BF16),
        scratch_shapes=[head_copy] * 6 + [pltpu.VMEM((seq, 2 * HEAD_DIM), F32),
                                          pltpu.VMEM((2, seq, 2 * HEAD_DIM), F32)],
        compiler_params=_cparams("parallel", "parallel"),
        name="dilated",
    )(_slope_pieces(slopes * LOG2E), proj, proj, proj, q_gain.reshape(1, -1), k_gain.reshape(1, -1),
      jnp.asarray(_dilated_near_log_weights()))


def _nsa_overlap_t(n_blocks, n_cmp_rows):
    blk = np.arange(n_blocks)[:, None]
    start = np.arange(n_cmp_rows)[None, :] * CMP_STRIDE
    return ((start < (blk + 1) * SLC_BLOCK) & (start + CMP_LEN > blk * SLC_BLOCK)).astype(np.float32)


def _nsa_kernel(pieces_ref, x_ref, qg_ref, kg_ref, pe_ref, w1_ref, w2_ref, ovl_ref, o_ref,
                qn_ref, ks_ref, vs_ref, kw_ref, vw_ref, kc_ref, vc_ref, rc_slc_ref, rc_win_ref, *, seq):
    tile = NSA_TILE
    n_tiles = seq // tile
    n_cmp = seq // CMP_STRIDE
    n_blk = seq // SLC_BLOCK
    grp = pl.program_id(1)
    heads = range(C_GROUP)

    prep_rows = min(NSA_PREP_ROWS, seq)
    q_cols = [_alibi_query_cols([pieces_ref[(grp * C_GROUP + j) * 3 + n] for n in range(3)], prep_rows)
              for j in heads]
    row_pos = lax.broadcasted_iota(jnp.int32, (prep_rows, HEAD_DIM), 0)

    def prep(i, carry):
        rows = pl.ds(pl.multiple_of(i * prep_rows, prep_rows), prep_rows)
        k_cols = _alibi_key_cols(i * prep_rows + row_pos)
        x = x_ref[rows, :NSA_GATE_COL].astype(F32)
        for j in heads:
            qj = _rms(x[:, j * HEAD_DIM:(j + 1) * HEAD_DIM], qg_ref[...]) * (ATTN_SCALE * LOG2E)
            qj = jnp.concatenate([qj, q_cols[j]], axis=1).astype(BF16)
            for t in range(prep_rows // tile):
                at = ((i * (prep_rows // tile) + t) * C_GROUP + j) * tile
                qn_ref[pl.ds(pl.multiple_of(at, tile), tile), :] = qj[t * tile:(t + 1) * tile]
        kc_ref[rows, :] = x[:, 256:320]
        vc_ref[rows, :] = x[:, 320:384]
        ks_ref[rows, :] = jnp.concatenate([_rms(x[:, 384:448], kg_ref[1:2]), k_cols], axis=1).astype(BF16)
        vs_ref[rows, :] = _with_ones(x[:, 448:512]).astype(BF16)
        kw_ref[rows, :] = jnp.concatenate([_rms(x[:, 512:576], kg_ref[2:3]), k_cols], axis=1).astype(BF16)
        vw_ref[rows, :] = _with_ones(x[:, 576:640]).astype(BF16)
        return carry

    lax.fori_loop(0, seq // prep_rows, prep, 0)
    kc_ref[seq:, :] = jnp.zeros((CMP_LEN, HEAD_DIM), F32)
    vc_ref[seq:, :] = jnp.zeros((CMP_LEN, HEAD_DIM), F32)

    def compress(src_ref, which):
        hid = jnp.zeros((n_cmp, CMP_HIDDEN), F32)
        for pos in range(CMP_LEN):
            tok = src_ref[pl.ds(pos, n_cmp, stride=CMP_STRIDE), :] + pe_ref[which, pos:pos + 1, :]
            hid = hid + _dot(tok.astype(BF16), w1_ref[which, pos * HEAD_DIM:(pos + 1) * HEAD_DIM, :])
        return _dot(_silu(hid).astype(BF16), w2_ref[which])

    cmp_pos = lax.broadcasted_iota(jnp.int32, (n_cmp, HEAD_DIM), 0) * CMP_STRIDE + (CMP_LEN - 1)
    k_cmp = jnp.concatenate([_rms(compress(kc_ref, 0), kg_ref[0:1]), _alibi_key_cols(cmp_pos)],
                            axis=1).astype(BF16)
    v_cmp = compress(vc_ref, 1).astype(BF16)
    ovl = ovl_ref[...].astype(BF16)

    rc_slc_ref[...] = (lax.broadcasted_iota(jnp.int32, (tile, SLC_CHUNK), 0)
                       - lax.broadcasted_iota(jnp.int32, (tile, SLC_CHUNK), 1)).astype(F32)
    rc_win_ref[...] = (lax.broadcasted_iota(jnp.int32, (tile, WIN + tile), 0)
                       - lax.broadcasted_iota(jnp.int32, (tile, WIN + tile), 1)).astype(F32)
    blk_row = lax.broadcasted_iota(jnp.int32, (tile, SLC_CHUNK), 0)
    blk_of_key = lax.broadcasted_iota(jnp.int32, (tile, SLC_CHUNK), 1) // SLC_BLOCK
    cmp_end = (lax.broadcasted_iota(jnp.int32, (tile, n_cmp), 1) * CMP_STRIDE + (CMP_LEN - 1))
    blk_i = lax.broadcasted_iota(jnp.int32, (n_blk, tile), 0)
    tok_lane = lax.broadcasted_iota(jnp.int32, (n_blk, tile), 1)

    def head_rows(s):
        return [s[j * tile:(j + 1) * tile] for j in heads]

    def q_tile(i, carry):
        t0 = i * tile
        rows = pl.ds(pl.multiple_of(t0, tile), tile)
        q4 = qn_ref[pl.ds(pl.multiple_of(i * C_GROUP * tile, tile), C_GROUP * tile), :]

        vis_c = (t0 + lax.broadcasted_iota(jnp.int32, (tile, n_cmp), 0)) >= cmp_end
        a_c = [jnp.where(vis_c, x, NEG) for x in head_rows(_dot_nt(q4, k_cmp))]
        m_c = [jnp.max(x, axis=-1, keepdims=True) for x in a_c]
        e_c = [jnp.where(vis_c, jnp.exp2(x - m), 0.0) for x, m in zip(a_c, m_c)]
        l_c = [jnp.sum(x, axis=-1, keepdims=True) for x in e_c]
        p_c = [x / jnp.where(l > 0, l, 1.0) for x, l in zip(e_c, l_c)]
        o_cmp = [_dot(x.astype(BF16), v_cmp) for x in p_c]
        hi, mid, lo = _split3(p_c[0] + p_c[1] + p_c[2] + p_c[3])
        imp = _dot_nt(ovl, hi) + _dot_nt(ovl, mid) + _dot_nt(ovl, lo)

        cur = lax.shift_right_logical(t0 + tok_lane, 6)
        forced = (blk_i == 0) | (blk_i == cur) | (blk_i == cur - 1)
        score = jnp.where(blk_i > cur, -1.0, jnp.where(forced, 1e30, imp))
        rank = jnp.zeros((n_blk, tile), F32)
        for other in range(n_blk):
            s_o = score[other:other + 1, :]
            rank = rank + jnp.where(s_o > score, 1.0,
                                    jnp.where(s_o == score, jnp.where(blk_i > other, 1.0, 0.0), 0.0))
        sel = jnp.where(rank < SLC_TOPN, jnp.where(score >= 0, 1.0, 0.0), 0.0)
        sel_t = jnp.concatenate([sel, jnp.zeros((tile - n_blk, tile), F32)], axis=0).T.astype(BF16)

        def slc_step(c, st):
            krows = pl.ds(pl.multiple_of(c * SLC_CHUNK, SLC_CHUNK), SLC_CHUNK)
            expand = jnp.where(blk_row == c * (SLC_CHUNK // SLC_BLOCK) + blk_of_key, 1.0, 0.0).astype(BF16)
            picked = _dot(sel_t, expand)
            dist = rc_slc_ref[...] + jnp.asarray(t0 - c * SLC_CHUNK, F32)
            mask = jnp.where(picked > 0.5, jnp.where(dist >= 0, 0.0, NEG), NEG)
            a = [x + mask for x in head_rows(_dot_nt(q4, ks_ref[krows, :]))]
            return _online_softmax_of(a, vs_ref[krows, :], st)

        init = tuple((jnp.full((tile, 1), NEG, F32), jnp.zeros((tile, 2 * HEAD_DIM), F32)) for _ in heads)
        slc = lax.fori_loop(0, (t0 + tile - 1) // SLC_CHUNK + 1, slc_step, init)

        first = jnp.maximum(i - WIN // tile, 0)
        krows = pl.ds(pl.multiple_of(first * tile, tile), WIN + tile)
        dist = rc_win_ref[...] + jnp.asarray((i - first) * tile, F32)
        mask = jnp.where(dist >= 0, jnp.where(dist < WIN, 0.0, NEG), NEG)
        win = _softmax_of([x + mask for x in head_rows(_dot_nt(q4, kw_ref[krows, :]))],
                          [vw_ref[krows, :]] * C_GROUP)

        gate = jax.nn.sigmoid(x_ref[rows, NSA_GATE_COL:NSA_GATE_COL + 128].astype(F32))
        outs = []
        for j in heads:
            outs.append(gate[:, j:j + 1] * o_cmp[j]
                        + gate[:, C_GROUP + j:C_GROUP + j + 1] * _normalised(slc[j][1])
                        + gate[:, 2 * C_GROUP + j:2 * C_GROUP + j + 1] * _normalised(win[j][1]))
        o_ref[rows, :] = jnp.concatenate(outs, axis=1).astype(BF16)
        return carry

    lax.fori_loop(0, n_tiles, q_tile, 0)


def _nsa(proj, q_gain, k_gain, cmp_pe, cmp_w1, cmp_w2, *, batch, seq):
    heads = C_KV_HEADS * C_GROUP
    slopes = (2.0 ** (-8.0 * jnp.arange(1, heads + 1, dtype=F32) / heads)).astype(F32)
    n_cmp = seq // CMP_STRIDE
    n_blk = seq // SLC_BLOCK
    assert seq % SLC_CHUNK == 0 and seq >= WIN + NSA_TILE and n_blk <= NSA_TILE and n_cmp % 128 == 0
    assert SLC_CHUNK % NSA_TILE == 0 and WIN % NSA_TILE == 0
    full = lambda *shape: pl.BlockSpec(shape, lambda b, g: (0,) * len(shape))
    head_copy = pltpu.VMEM((seq, 2 * HEAD_DIM), BF16)
    return pl.pallas_call(
        functools.partial(_nsa_kernel, seq=seq),
        grid=(batch, C_KV_HEADS),
        in_specs=[pl.BlockSpec(memory_space=pltpu.SMEM),
                  pl.BlockSpec((seq, NSA_GROUP_COLS), lambda b, g: (b, g)),
                  full(1, HEAD_DIM), full(3, HEAD_DIM), full(2, CMP_LEN, HEAD_DIM),
                  full(2, CMP_LEN * HEAD_DIM, CMP_HIDDEN), full(2, CMP_HIDDEN, HEAD_DIM),
                  full(n_blk, n_cmp)],
        out_specs=pl.BlockSpec((seq, C_GROUP * HEAD_DIM), lambda b, g: (b, g)),
        out_shape=jax.ShapeDtypeStruct((batch * seq, heads * HEAD_DIM), BF16),
        scratch_shapes=[pltpu.VMEM((seq * C_GROUP, 2 * HEAD_DIM), BF16)] + [head_copy] * 4 + [
            pltpu.VMEM((seq + CMP_LEN, HEAD_DIM), F32),
            pltpu.VMEM((seq + CMP_LEN, HEAD_DIM), F32),
            pltpu.VMEM((NSA_TILE, SLC_CHUNK), F32),
            pltpu.VMEM((NSA_TILE, WIN + NSA_TILE), F32)],
        compiler_params=_cparams("parallel", "parallel"),
        name="nsa",
    )(_slope_pieces(slopes * LOG2E), proj, q_gain.reshape(1, -1), k_gain, cmp_pe,
      cmp_w1.astype(BF16), cmp_w2.astype(BF16), jnp.asarray(_nsa_overlap_t(n_blk, n_cmp)))


def _nsa_in_weight(w_in, d_model):
    kv_w = C_KV_HEADS * HEAD_DIM
    cols = []
    for g in range(C_KV_HEADS):
        idx = list(range(g * C_GROUP * HEAD_DIM, (g + 1) * C_GROUP * HEAD_DIM))
        for part in range(6):
            base = d_model + part * kv_w + g * HEAD_DIM
            idx += list(range(base, base + HEAD_DIM))
        gate0 = d_model + 6 * kv_w
        idx += [gate0 + n * C_KV_HEADS * C_GROUP + g * C_GROUP + j for n in range(3) for j in range(C_GROUP)]
        cols.append(jnp.pad(w_in[:, np.asarray(idx)], ((0, 0), (0, NSA_GROUP_COLS - len(idx)))))
    return jnp.concatenate(cols, axis=1)


def kernel(x, attn_norm, ffn_norm, hy_w_in, hy_lb_logits, hy_o_gain, hy_q_gain, hy_k_gain, hy_w_out, nsa_w_in, nsa_q_gain, nsa_k_gain, nsa_cmp_pe, nsa_cmp_w1, nsa_cmp_w2, nsa_w_out, ffn_w_gate, ffn_w_up, ffn_w_down):
    batch, seq, d_model = x.shape
    depth = attn_norm.shape[0]
    a_width = hy_o_gain.shape[1]
    dff = ffn_w_gate.shape[2]
    h = x.reshape(batch * seq, d_model)
    for layer in range(depth):
        i = layer // 2
        if layer % 2 == 0:
            proj = _norm_matmul(h, attn_norm[layer], hy_w_in[i].astype(BF16), tm=1024,
                                tn=hy_w_in.shape[2] // 2)
            a_out = _hgrn2(proj, hy_lb_logits, hy_o_gain[i], layer_idx=i, batch=batch, seq=seq,
                           heads_per_step=4, chunks_per_step=8)
            b_out = _dilated(proj, hy_q_gain[i], hy_k_gain[i], batch=batch, seq=seq,
                             col0=4 * a_width // (2 * HEAD_DIM))
            w_out = hy_w_out[i].astype(BF16)
            mixed, w_mix = [a_out, b_out], [w_out[:a_width], w_out[a_width:]]
        else:
            proj = _norm_matmul(h, attn_norm[layer], _nsa_in_weight(nsa_w_in[i], d_model).astype(BF16),
                                tm=1024, tn=2 * NSA_GROUP_COLS)
            o = _nsa(proj, nsa_q_gain[i], nsa_k_gain[i], nsa_cmp_pe[i], nsa_cmp_w1[i], nsa_cmp_w2[i],
                     batch=batch, seq=seq)
            mixed, w_mix = [o], [nsa_w_out[i].astype(BF16)]
        h = _mix_ffn(h, mixed, w_mix, ffn_norm[layer], ffn_w_gate[layer].astype(BF16),
                     ffn_w_up[layer].astype(BF16), ffn_w_down[layer].astype(BF16), tm=1024, tf=dff // 2)
    return h.reshape(batch, seq, d_model)
```
